```python
import jax, jax.numpy as jnp
from jax import lax
import numpy as np

D_MODEL = 1024
BATCH = 2
SEQ = 8192
DEPTH = 1

ATTN_WIDTH = D_MODEL // 2
POOL_WIDTH = D_MODEL - ATTN_WIDTH
HEAD_DIM = 64
N_Q_HEADS = ATTN_WIDTH // HEAD_DIM
N_KV_HEADS = 2
GQA_GROUP = N_Q_HEADS // N_KV_HEADS
KV_WIDTH = N_KV_HEADS * HEAD_DIM
WINDOW = 128
BLOCK = 128
ROPE_THETA = 10000.0
POOL_SIZES = (2, 4, 8, 16)
N_POOL_GROUPS = len(POOL_SIZES)
POOL_GROUP_WIDTH = POOL_WIDTH // N_POOL_GROUPS
IN_WIDTH = ATTN_WIDTH + 2 * KV_WIDTH + POOL_WIDTH
D_FF = -(-8 * D_MODEL // (3 * 256)) * 256
RMS_EPS = 1e-5

kernel_name = "hybrid_swa_sink_multiscale_pool_block"


def rmsnorm(x, g):
    xf = x.astype(jnp.float32)
    y = xf * lax.rsqrt(jnp.mean(xf * xf, axis=-1, keepdims=True) + RMS_EPS)
    return (y * g.astype(jnp.float32)).astype(x.dtype)


def rope_tables(seq):
    inv_freq = 1.0 / (ROPE_THETA ** (jnp.arange(0, HEAD_DIM, 2, dtype=jnp.float32) / HEAD_DIM))
    ang = jnp.arange(seq, dtype=jnp.float32)[:, None] * inv_freq[None, :]
    return jnp.cos(ang), jnp.sin(ang)


def apply_rope(t, cos, sin):
    t1, t2 = jnp.split(t.astype(jnp.float32), 2, axis=-1)
    c = cos[None, :, None, :]
    s = sin[None, :, None, :]
    return jnp.concatenate([t1 * c - t2 * s, t2 * c + t1 * s], axis=-1).astype(t.dtype)


def sliding_window_attention_with_sinks(q, k, v, sinks):
    b, s = q.shape[0], q.shape[1]
    nb = s // BLOCK
    qb = q.reshape(b, nb, BLOCK, N_KV_HEADS, GQA_GROUP, HEAD_DIM)

    def band(t):
        t = t.reshape(b, nb, BLOCK, N_KV_HEADS, HEAD_DIM)
        prev = jnp.pad(t, ((0, 0), (1, 0), (0, 0), (0, 0), (0, 0)))[:, :-1]
        return jnp.concatenate([prev, t], axis=2)

    kb, vb = band(k), band(v)
    scores = jnp.einsum('bnqkgd,bnskd->bnkgqs', qb, kb,
                        preferred_element_type=jnp.float32) * (HEAD_DIM ** -0.5)
    qi = jnp.arange(BLOCK)[:, None] + BLOCK
    sj = jnp.arange(2 * BLOCK)[None, :]
    delta = qi - sj
    in_window = (delta >= 0) & (delta < WINDOW)
    key_pos = jnp.arange(nb)[:, None] * BLOCK + sj - BLOCK
    mask = in_window[None] & (key_pos >= 0)[:, None, :]
    scores = jnp.where(mask[None, :, None, None], scores, -jnp.inf)
    sink = sinks.astype(jnp.float32).reshape(N_KV_HEADS, GQA_GROUP)[None, None, :, :, None, None]
    m = jnp.maximum(jnp.max(scores, axis=-1, keepdims=True), sink)
    p = jnp.exp(scores - m)
    p = p / (jnp.sum(p, axis=-1, keepdims=True) + jnp.exp(sink - m))
    out = jnp.einsum('bnkgqs,bnskd->bnqkgd', p.astype(v.dtype), vb)
    return out.reshape(b, s, N_Q_HEADS * HEAD_DIM)


def multiscale_causal_pool(u, w_pool, b_pool, pool_scale):
    b, s, _ = u.shape
    uf = u.astype(jnp.float32).reshape(b, s, N_POOL_GROUPS, POOL_GROUP_WIDTH)
    cs = jnp.pad(jnp.cumsum(uf, axis=1), ((0, 0), (1, 0), (0, 0), (0, 0)))
    t = jnp.arange(s)[:, None]
    sizes = jnp.array(POOL_SIZES, dtype=jnp.int32)[None, :]
    start = jnp.maximum(t + 1 - sizes, 0)
    g_idx = jnp.arange(N_POOL_GROUPS)[None, :]
    window_sum = cs[:, 1:] - cs[:, start, g_idx]
    count = (t + 1 - start).astype(jnp.float32)
    mixed = window_sum / count[None, :, :, None] - uf
    y = jnp.einsum('bsgc,gcd->bsgd', mixed.astype(u.dtype), w_pool) + b_pool
    y = y * pool_scale
    return y.reshape(b, s, POOL_WIDTH)


def setup_inputs(seed: int = 0) -> dict:
    key = jax.random.key(seed)
    ks = jax.random.split(key, 16)
    f32 = jnp.float32
    nrm = lambda k, shape, scale: jax.random.normal(k, shape, f32) * scale
    return {
        "x": nrm(ks[0], (BATCH, SEQ, D_MODEL), 1.0),
        "g_mix": 1.0 + nrm(ks[1], (DEPTH, D_MODEL), 0.02),
        "w_in": nrm(ks[2], (DEPTH, D_MODEL, IN_WIDTH), D_MODEL ** -0.5),
        "b_in": nrm(ks[3], (DEPTH, IN_WIDTH), 0.02),
        "sinks": nrm(ks[4], (DEPTH, N_Q_HEADS), 1.0),
        "w_pool": nrm(ks[5], (DEPTH, N_POOL_GROUPS, POOL_GROUP_WIDTH, POOL_GROUP_WIDTH), POOL_GROUP_WIDTH ** -0.5),
        "b_pool": nrm(ks[6], (DEPTH, N_POOL_GROUPS, POOL_GROUP_WIDTH), 0.02),
        "pool_scale": 1.0 + nrm(ks[7], (DEPTH, N_POOL_GROUPS, POOL_GROUP_WIDTH), 0.1),
        "w_out": nrm(ks[8], (DEPTH, ATTN_WIDTH + POOL_WIDTH, D_MODEL), (ATTN_WIDTH + POOL_WIDTH) ** -0.5),
        "b_out": nrm(ks[9], (DEPTH, D_MODEL), 0.02),
        "g_ffn": 1.0 + nrm(ks[10], (DEPTH, D_MODEL), 0.02),
        "w_gate": nrm(ks[11], (DEPTH, D_MODEL, D_FF), D_MODEL ** -0.5),
        "w_up": nrm(ks[12], (DEPTH, D_MODEL, D_FF), D_MODEL ** -0.5),
        "w_down": nrm(ks[13], (DEPTH, D_FF, D_MODEL), D_FF ** -0.5),
        "g_final": 1.0 + nrm(ks[14], (D_MODEL,), 0.02),
    }


def reference(x, g_mix, w_in, b_in, sinks, w_pool, b_pool, pool_scale, w_out, b_out,
              g_ffn, w_gate, w_up, w_down, g_final):
    b, s, _ = x.shape
    cos, sin = rope_tables(s)
    for i in range(DEPTH):
        h = rmsnorm(x, g_mix[i])
        z = h @ w_in[i] + b_in[i]
        q = z[..., :ATTN_WIDTH].reshape(b, s, N_Q_HEADS, HEAD_DIM)
        k = z[..., ATTN_WIDTH:ATTN_WIDTH + KV_WIDTH].reshape(b, s, N_KV_HEADS, HEAD_DIM)
        v = z[..., ATTN_WIDTH + KV_WIDTH:ATTN_WIDTH + 2 * KV_WIDTH].reshape(b, s, N_KV_HEADS, HEAD_DIM)
        u = z[..., ATTN_WIDTH + 2 * KV_WIDTH:]
        q = apply_rope(q, cos, sin)
        k = apply_rope(k, cos, sin)
        attn = sliding_window_attention_with_sinks(q, k, v, sinks[i])
        pool = multiscale_causal_pool(u, w_pool[i], b_pool[i], pool_scale[i])
        x = x + jnp.concatenate([attn, pool], axis=-1) @ w_out[i] + b_out[i]
        h = rmsnorm(x, g_ffn[i])
        x = x + (jax.nn.silu(h @ w_gate[i]) * (h @ w_up[i])) @ w_down[i]
    return rmsnorm(x, g_final)
```

```python
import functools

import jax
import jax.numpy as jnp
from jax import lax
from jax.experimental import pallas as pl
from jax.experimental.pallas import tpu as pltpu

D_MODEL = 1024
HEAD_DIM = 64
HALF_HEAD = HEAD_DIM // 2
N_Q_HEADS = 8
N_KV_HEADS = 2
GQA_GROUP = N_Q_HEADS // N_KV_HEADS
ATTN_WIDTH = N_Q_HEADS * HEAD_DIM
KV_WIDTH = N_KV_HEADS * HEAD_DIM
POOL_SIZES = (2, 4, 8, 16)
N_POOL_GROUPS = len(POOL_SIZES)
POOL_GROUP_WIDTH = 128
POOL_WIDTH = N_POOL_GROUPS * POOL_GROUP_WIDTH
IN_WIDTH = ATTN_WIDTH + 2 * KV_WIDTH + POOL_WIDTH
WINDOW = 128
BLOCK = 128
ROPE_THETA = 10000.0
RMS_EPS = 1e-5
POOL_HALO = 16

LANES = 128
VMEM_LIMIT_BYTES = 56 * 1024 * 1024

MIXER_ROWS = 256
FFN_ROWS = 256


def _rmsnorm(x, g):
    ms = jnp.mean(x * x, axis=-1, keepdims=True)
    return x * lax.rsqrt(ms + RMS_EPS) * g


def _mixer_kernel(sinks_ref, x_ref, g_ref, win_ref, bin_ref, wpool_ref, bpool_ref, pscale_ref,
                  wout_ref, bout_ref, cos_ref, sin_ref, o_ref, kext_ref, vext_ref, utail_ref):
    rows = x_ref.shape[0]
    i = pl.program_id(1)

    @pl.when(i == 0)
    def _():
        kext_ref[:, 0:BLOCK, :] = jnp.zeros((N_KV_HEADS, BLOCK, LANES), kext_ref.dtype)
        vext_ref[:, 0:BLOCK, :] = jnp.zeros((N_KV_HEADS, BLOCK, LANES), vext_ref.dtype)
        utail_ref[...] = jnp.zeros(utail_ref.shape, utail_ref.dtype)

    x = x_ref[...]
    h = _rmsnorm(x, g_ref[...]).astype(jnp.bfloat16)
    z = jnp.dot(h, win_ref[...], preferred_element_type=jnp.float32) + bin_ref[...]

    lane = lax.broadcasted_iota(jnp.int32, (rows, LANES), 1)
    low_half = lane < HEAD_DIM
    first_half = (lane % HEAD_DIM) < HALF_HEAD
    cos = cos_ref[...]
    sin = sin_ref[...]

    def rope(t):
        partner = jnp.where(first_half, pltpu.roll(t, LANES - HALF_HEAD, 1), pltpu.roll(t, HALF_HEAD, 1))
        return t * cos + partner * sin

    k = rope(z[:, ATTN_WIDTH:ATTN_WIDTH + KV_WIDTH])
    v = z[:, ATTN_WIDTH + KV_WIDTH:ATTN_WIDTH + 2 * KV_WIDTH]
    k_sw = pltpu.roll(k, HEAD_DIM, 1)
    v_sw = pltpu.roll(v, HEAD_DIM, 1)
    kext_ref[0, BLOCK:, :] = jnp.where(low_half, k, k_sw).astype(kext_ref.dtype)
    kext_ref[1, BLOCK:, :] = jnp.where(low_half, k_sw, k).astype(kext_ref.dtype)
    vext_ref[0, BLOCK:, :] = jnp.where(low_half, v, v_sw).astype(vext_ref.dtype)
    vext_ref[1, BLOCK:, :] = jnp.where(low_half, v_sw, v).astype(vext_ref.dtype)

    scale = HEAD_DIM ** -0.5
    q_pairs = [rope(z[:, p * LANES:(p + 1) * LANES]) * scale for p in range(ATTN_WIDTH // LANES)]

    qi = lax.broadcasted_iota(jnp.int32, (BLOCK, 2 * BLOCK), 0) + BLOCK
    sj = lax.broadcasted_iota(jnp.int32, (BLOCK, 2 * BLOCK), 1)
    delta = qi - sj
    in_window = (delta >= 0) & (delta < WINDOW)
    low_b = lax.broadcasted_iota(jnp.int32, (BLOCK, LANES), 1) < HEAD_DIM

    attn_blocks = []
    for j in range(rows // BLOCK):
        key_pos = i * rows + j * BLOCK + sj - BLOCK
        bias = jnp.where(in_window & (key_pos >= 0), 0.0, -jnp.inf).astype(jnp.float32)
        pair_outs = []
        for g in range(N_KV_HEADS):
            kband = kext_ref[g, j * BLOCK:(j + 2) * BLOCK, :]
            vband = vext_ref[g, j * BLOCK:(j + 2) * BLOCK, :]
            qm = []
            for p in range(GQA_GROUP // 2):
                qp = q_pairs[g * (GQA_GROUP // 2) + p][j * BLOCK:(j + 1) * BLOCK]
                qm.append(jnp.where(low_b, qp, 0.0))
                qm.append(jnp.where(low_b, 0.0, qp))
            qm = jnp.concatenate(qm, axis=0).astype(jnp.bfloat16)
            s = lax.dot_general(qm, kband, (((1,), (1,)), ((), ())),
                                preferred_element_type=jnp.float32)
            ps, inv_ls = [], []
            for hh in range(GQA_GROUP):
                sink = sinks_ref[g * GQA_GROUP + hh]
                s_h = s[hh * BLOCK:(hh + 1) * BLOCK] + bias
                m = jnp.maximum(jnp.max(s_h, axis=-1, keepdims=True), sink)
                p_h = jnp.exp(s_h - m)
                l = jnp.sum(p_h, axis=-1, keepdims=True) + jnp.exp(sink - m)
                ps.append(p_h.astype(jnp.bfloat16))
                inv_ls.append(1.0 / l)
            pm = jnp.concatenate(ps, axis=0)
            o = jnp.dot(pm, vband, preferred_element_type=jnp.float32)
            for p in range(GQA_GROUP // 2):
                o_a = o[(2 * p) * BLOCK:(2 * p + 1) * BLOCK] * inv_ls[2 * p]
                o_b = o[(2 * p + 1) * BLOCK:(2 * p + 2) * BLOCK] * inv_ls[2 * p + 1]
                pair_outs.append(jnp.where(low_b, o_a, o_b))
        attn_blocks.append(jnp.concatenate(pair_outs, axis=1))
    attn = jnp.concatenate(attn_blocks, axis=0)

    kext_ref[:, 0:BLOCK, :] = kext_ref[:, rows:rows + BLOCK, :]
    vext_ref[:, 0:BLOCK, :] = vext_ref[:, rows:rows + BLOCK, :]

    u = z[:, ATTN_WIDTH + 2 * KV_WIDTH:]
    pos = i * rows + lax.broadcasted_iota(jnp.int32, (rows, LANES), 0)
    pool_groups = []
    for gi, size in enumerate(POOL_SIZES):
        u_g = u[:, gi * LANES:(gi + 1) * LANES]
        acc = jnp.concatenate([utail_ref[:, gi * LANES:(gi + 1) * LANES], u_g], axis=0)
        span = 1
        while span < size:
            acc = acc + pltpu.roll(acc, span, 0)
            span *= 2
        count = jnp.minimum(pos + 1, size).astype(jnp.float32)
        mixed = acc[POOL_HALO:] / count - u_g
        y = jnp.dot(mixed.astype(jnp.bfloat16), wpool_ref[gi], preferred_element_type=jnp.float32)
        pool_groups.append(y)
    utail_ref[...] = u[rows - POOL_HALO:]
    pool = (jnp.concatenate(pool_groups, axis=1) + bpool_ref[...]) * pscale_ref[...]

    cat = jnp.concatenate([attn, pool], axis=1).astype(jnp.bfloat16)
    o_ref[...] = x + jnp.dot(cat, wout_ref[...], preferred_element_type=jnp.float32) + bout_ref[...]


def _ffn_kernel(x_ref, g_ref, wg_ref, wu_ref, wd_ref, gf_ref, o_ref, *, final_norm):
    x = x_ref[...]
    h = _rmsnorm(x, g_ref[...]).astype(jnp.bfloat16)
    gate = jnp.dot(h, wg_ref[...], preferred_element_type=jnp.float32)
    up = jnp.dot(h, wu_ref[...], preferred_element_type=jnp.float32)
    act = (gate * jax.nn.sigmoid(gate) * up).astype(jnp.bfloat16)
    y = x + jnp.dot(act, wd_ref[...], preferred_element_type=jnp.float32)
    o_ref[...] = _rmsnorm(y, gf_ref[...]) if final_norm else y


def _resident(shape):
    zeros = (0,) * len(shape)
    return pl.BlockSpec(shape, lambda *_: zeros, pipeline_mode=pl.Buffered(1))


def _rope_tables(seq):
    inv_freq = 1.0 / (ROPE_THETA ** (jnp.arange(0, HEAD_DIM, 2, dtype=jnp.float32) / HEAD_DIM))
    ang = jnp.arange(seq, dtype=jnp.float32)[:, None] * inv_freq[None, :]
    cos, sin = jnp.cos(ang), jnp.sin(ang)
    return jnp.tile(cos, (1, 4)), jnp.concatenate([-sin, sin, -sin, sin], axis=1)


def _mixer(x, g_mix, w_in, b_in, sinks, w_pool, b_pool, pool_scale, w_out, b_out):
    b, s, d = x.shape
    rows = MIXER_ROWS
    assert s % rows == 0 and rows % BLOCK == 0 and d == D_MODEL
    cos_t, sin_t = _rope_tables(s)
    bf16 = jnp.bfloat16
    grid_spec = pltpu.PrefetchScalarGridSpec(
        num_scalar_prefetch=1,
        grid=(b, s // rows),
        in_specs=[
            pl.BlockSpec((None, rows, d), lambda bi, i, *_: (bi, i, 0)),
            _resident((1, d)),
            _resident((d, IN_WIDTH)),
            _resident((1, IN_WIDTH)),
            _resident((N_POOL_GROUPS, POOL_GROUP_WIDTH, POOL_GROUP_WIDTH)),
            _resident((1, POOL_WIDTH)),
            _resident((1, POOL_WIDTH)),
            _resident((ATTN_WIDTH + POOL_WIDTH, d)),
            _resident((1, d)),
            pl.BlockSpec((rows, LANES), lambda bi, i, *_: (i, 0)),
            pl.BlockSpec((rows, LANES), lambda bi, i, *_: (i, 0)),
        ],
        out_specs=pl.BlockSpec((None, rows, d), lambda bi, i, *_: (bi, i, 0)),
        scratch_shapes=[
            pltpu.VMEM((N_KV_HEADS, rows + BLOCK, LANES), bf16),
            pltpu.VMEM((N_KV_HEADS, rows + BLOCK, LANES), bf16),
            pltpu.VMEM((POOL_HALO, POOL_WIDTH), jnp.float32),
        ],
    )
    return pl.pallas_call(
        _mixer_kernel,
        grid_spec=grid_spec,
        out_shape=jax.ShapeDtypeStruct(x.shape, x.dtype),
        compiler_params=pltpu.CompilerParams(
            dimension_semantics=("arbitrary", "arbitrary"),
            vmem_limit_bytes=VMEM_LIMIT_BYTES),
        name="mixer",
    )(sinks.astype(jnp.float32), x, g_mix.reshape(1, d), w_in.astype(bf16), b_in.reshape(1, IN_WIDTH),
      w_pool.astype(bf16), b_pool.reshape(1, POOL_WIDTH), pool_scale.reshape(1, POOL_WIDTH),
      w_out.astype(bf16), b_out.reshape(1, d), cos_t, sin_t)


def _ffn(x, g_ffn, w_gate, w_up, w_down, g_final, *, final_norm):
    b, s, d = x.shape
    rows = FFN_ROWS
    tokens = b * s
    assert tokens % rows == 0
    d_ff = w_gate.shape[-1]
    bf16 = jnp.bfloat16
    out = pl.pallas_call(
        functools.partial(_ffn_kernel, final_norm=final_norm),
        grid=(tokens // rows,),
        in_specs=[
            pl.BlockSpec((rows, d), lambda i: (i, 0)),
            _resident((1, d)),
            _resident((d, d_ff)),
            _resident((d, d_ff)),
            _resident((d_ff, d)),
            _resident((1, d)),
        ],
        out_specs=pl.BlockSpec((rows, d), lambda i: (i, 0)),
        out_shape=jax.ShapeDtypeStruct((tokens, d), x.dtype),
        compiler_params=pltpu.CompilerParams(
            dimension_semantics=("arbitrary",),
            vmem_limit_bytes=VMEM_LIMIT_BYTES),
        name="ffn",
    )(x.reshape(tokens, d), g_ffn.reshape(1, d), w_gate.astype(bf16), w_up.astype(bf16),
      w_down.astype(bf16), g_final.reshape(1, d))
    return out.reshape(b, s, d)


def kernel(x, g_mix, w_in, b_in, sinks, w_pool, b_pool, pool_scale, w_out, b_out, g_ffn, w_gate, w_up, w_down, g_final):
    depth = g_mix.shape[0]
    for layer in range(depth):
        x = _mixer(x, g_mix[layer], w_in[layer], b_in[layer], sinks[layer], w_pool[layer],
                   b_pool[layer], pool_scale[layer], w_out[layer], b_out[layer])
        x = _ffn(x, g_ffn[layer], w_gate[layer], w_up[layer], w_down[layer], g_final,
                 final_norm=layer == depth - 1)
    return x
```

```python
import functools

import jax
import jax.numpy as jnp
from jax import lax
from jax.experimental import pallas as pl
from jax.experimental.pallas import tpu as pltpu

D_MODEL = 1024
HEAD_DIM = 64
HALF_HEAD = HEAD_DIM // 2
N_Q_HEADS = 8
N_KV_HEADS = 2
GQA_GROUP = N_Q_HEADS // N_KV_HEADS
ATTN_WIDTH = N_Q_HEADS * HEAD_DIM
KV_WIDTH = N_KV_HEADS * HEAD_DIM
POOL_SIZES = (2, 4, 8, 16)
N_POOL_GROUPS = len(POOL_SIZES)
POOL_GROUP_WIDTH = 128
POOL_WIDTH = N_POOL_GROUPS * POOL_GROUP_WIDTH
IN_WIDTH = ATTN_WIDTH + 2 * KV_WIDTH + POOL_WIDTH
WINDOW = 128
BLOCK = 128
ROPE_THETA = 10000.0
RMS_EPS = 1e-5
POOL_HALO = 16

LANES = 128
VMEM_LIMIT_BYTES = 56 * 1024 * 1024

MIXER_ROWS = 512
FFN_ROWS = 512


def _rmsnorm(x, g):
    ms = jnp.mean(x * x, axis=-1, keepdims=True)
    return x * lax.rsqrt(ms + RMS_EPS) * g


def _mixer_kernel(sinks_ref, x_ref, g_ref, win_ref, bin_ref, wpool_ref, bpool_ref, pscale_ref,
                  wout_ref, bout_ref, cos_ref, sin_ref, o_ref, kext_ref, vext_ref, utail_ref):
    rows = x_ref.shape[0]
    i = pl.program_id(1)

    @pl.when(i == 0)
    def _():
        kext_ref[:, 0:BLOCK, :] = jnp.zeros((N_KV_HEADS, BLOCK, LANES), kext_ref.dtype)
        vext_ref[:, 0:BLOCK, :] = jnp.zeros((N_KV_HEADS, BLOCK, LANES), vext_ref.dtype)
        utail_ref[...] = jnp.zeros(utail_ref.shape, utail_ref.dtype)

    x = x_ref[...]
    h = _rmsnorm(x, g_ref[...]).astype(jnp.bfloat16)
    z = jnp.dot(h, win_ref[...], preferred_element_type=jnp.float32) + bin_ref[...]

    lane = lax.broadcasted_iota(jnp.int32, (rows, LANES), 1)
    low_half = lane < HEAD_DIM
    first_half = (lane % HEAD_DIM) < HALF_HEAD
    cos = cos_ref[...]
    sin = sin_ref[...]

    def rope(t):
        partner = jnp.where(first_half, pltpu.roll(t, LANES - HALF_HEAD, 1), pltpu.roll(t, HALF_HEAD, 1))
        return t * cos + partner * sin

    k = rope(z[:, ATTN_WIDTH:ATTN_WIDTH + KV_WIDTH])
    v = z[:, ATTN_WIDTH + KV_WIDTH:ATTN_WIDTH + 2 * KV_WIDTH]
    k_sw = pltpu.roll(k, HEAD_DIM, 1)
    v_sw = pltpu.roll(v, HEAD_DIM, 1)
    kext_ref[0, BLOCK:, :] = jnp.where(low_half, k, k_sw).astype(kext_ref.dtype)
    kext_ref[1, BLOCK:, :] = jnp.where(low_half, k_sw, k).astype(kext_ref.dtype)
    vext_ref[0, BLOCK:, :] = jnp.where(low_half, v, v_sw).astype(vext_ref.dtype)
    vext_ref[1, BLOCK:, :] = jnp.where(low_half, v_sw, v).astype(vext_ref.dtype)

    scale = HEAD_DIM ** -0.5
    q_pairs = [rope(z[:, p * LANES:(p + 1) * LANES]) * scale for p in range(ATTN_WIDTH // LANES)]

    qi = lax.broadcasted_iota(jnp.int32, (BLOCK, 2 * BLOCK), 0) + BLOCK
    sj = lax.broadcasted_iota(jnp.int32, (BLOCK, 2 * BLOCK), 1)
    delta = qi - sj
    in_window = (delta >= 0) & (delta < WINDOW)
    low_b = lax.broadcasted_iota(jnp.int32, (BLOCK, LANES), 1) < HEAD_DIM

    attn_blocks = []
    for j in range(rows // BLOCK):
        key_pos = i * rows + j * BLOCK + sj - BLOCK
        bias = jnp.where(in_window & (key_pos >= 0), 0.0, -jnp.inf).astype(jnp.float32)
        pair_outs = []
        for g in range(N_KV_HEADS):
            kband = kext_ref[g, j * BLOCK:(j + 2) * BLOCK, :]
            vband = vext_ref[g, j * BLOCK:(j + 2) * BLOCK, :]
            qm = []
            for p in range(GQA_GROUP // 2):
                qp = q_pairs[g * (GQA_GROUP // 2) + p][j * BLOCK:(j + 1) * BLOCK]
                qm.append(jnp.where(low_b, qp, 0.0))
                qm.append(jnp.where(low_b, 0.0, qp))
            qm = jnp.concatenate(qm, axis=0).astype(jnp.bfloat16)
            s = lax.dot_general(qm, kband, (((1,), (1,)), ((), ())),
                                preferred_element_type=jnp.float32)
            ps, inv_ls = [], []
            for hh in range(GQA_GROUP):
                sink = sinks_ref[g * GQA_GROUP + hh]
                s_h = s[hh * BLOCK:(hh + 1) * BLOCK] + bias
                m = jnp.maximum(jnp.max(s_h, axis=-1, keepdims=True), sink)
                p_h = jnp.exp(s_h - m)
                l = jnp.sum(p_h, axis=-1, keepdims=True) + jnp.exp(sink - m)
                ps.append(p_h.astype(jnp.bfloat16))
                inv_ls.append(1.0 / l)
            pm = jnp.concatenate(ps, axis=0)
            o = jnp.dot(pm, vband, preferred_element_type=jnp.float32)
            for p in range(GQA_GROUP // 2):
                o_a = o[(2 * p) * BLOCK:(2 * p + 1) * BLOCK] * inv_ls[2 * p]
                o_b = o[(2 * p + 1) * BLOCK:(2 * p + 2) * BLOCK] * inv_ls[2 * p + 1]
                pair_outs.append(jnp.where(low_b, o_a, o_b))
        attn_blocks.append(jnp.concatenate(pair_outs, axis=1))
    attn = jnp.concatenate(attn_blocks, axis=0)

    kext_ref[:, 0:BLOCK, :] = kext_ref[:, rows:rows + BLOCK, :]
    vext_ref[:, 0:BLOCK, :] = vext_ref[:, rows:rows + BLOCK, :]

    u = z[:, ATTN_WIDTH + 2 * KV_WIDTH:]
    pos = i * rows + lax.broadcasted_iota(jnp.int32, (rows, LANES), 0)
    pool_groups = []
    for gi, size in enumerate(POOL_SIZES):
        u_g = u[:, gi * LANES:(gi + 1) * LANES]
        acc = jnp.concatenate([utail_ref[:, gi * LANES:(gi + 1) * LANES], u_g], axis=0)
        span = 1
        while span < size:
            acc = acc + pltpu.roll(acc, span, 0)
            span *= 2
        count = jnp.minimum(pos + 1, size).astype(jnp.float32)
        mixed = acc[POOL_HALO:] / count - u_g
        y = jnp.dot(mixed.astype(jnp.bfloat16), wpool_ref[gi], preferred_element_type=jnp.float32)
        pool_groups.append(y)
    utail_ref[...] = u[rows - POOL_HALO:]
    pool = (jnp.concatenate(pool_groups, axis=1) + bpool_ref[...]) * pscale_ref[...]

    cat = jnp.concatenate([attn, pool], axis=1).astype(jnp.bfloat16)
    o_ref[...] = x + jnp.dot(cat, wout_ref[...], preferred_element_type=jnp.float32) + bout_ref[...]


def _ffn_kernel(x_ref, g_ref, wg_ref, wu_ref, wd_ref, gf_ref, o_ref, *, final_norm):
    x = x_ref[...]
    h = _rmsnorm(x, g_ref[...]).astype(jnp.bfloat16)
    gate = jnp.dot(h, wg_ref[...], preferred_element_type=jnp.float32)
    up = jnp.dot(h, wu_ref[...], preferred_element_type=jnp.float32)
    act = (gate * jax.nn.sigmoid(gate) * up).astype(jnp.bfloat16)
    y = x + jnp.dot(act, wd_ref[...], preferred_element_type=jnp.float32)
    o_ref[...] = _rmsnorm(y, gf_ref[...]) if final_norm else y


def _resident(shape):
    zeros = (0,) * len(shape)
    return pl.BlockSpec(shape, lambda *_: zeros, pipeline_mode=pl.Buffered(1))


def _rope_tables(seq):
    inv_freq = 1.0 / (ROPE_THETA ** (jnp.arange(0, HEAD_DIM, 2, dtype=jnp.float32) / HEAD_DIM))
    ang = jnp.arange(seq, dtype=jnp.float32)[:, None] * inv_freq[None, :]
    cos, sin = jnp.cos(ang), jnp.sin(ang)
    return jnp.tile(cos, (1, 4)), jnp.concatenate([-sin, sin, -sin, sin], axis=1)


def _mixer(x, g_mix, w_in, b_in, sinks, w_pool, b_pool, pool_scale, w_out, b_out):
    b, s, d = x.shape
    rows = MIXER_ROWS
    assert s % rows == 0 and rows % BLOCK == 0 and d == D_MODEL
    cos_t, sin_t = _rope_tables(s)
    bf16 = jnp.bfloat16
    grid_spec = pltpu.PrefetchScalarGridSpec(
        num_scalar_prefetch=1,
        grid=(b, s // rows),
        in_specs=[
            pl.BlockSpec((None, rows, d), lambda bi, i, *_: (bi, i, 0)),
            _resident((1, d)),
            _resident((d, IN_WIDTH)),
            _resident((1, IN_WIDTH)),
            _resident((N_POOL_GROUPS, POOL_GROUP_WIDTH, POOL_GROUP_WIDTH)),
            _resident((1, POOL_WIDTH)),
            _resident((1, POOL_WIDTH)),
            _resident((ATTN_WIDTH + POOL_WIDTH, d)),
            _resident((1, d)),
            pl.BlockSpec((rows, LANES), lambda bi, i, *_: (i, 0)),
            pl.BlockSpec((rows, LANES), lambda bi, i, *_: (i, 0)),
        ],
        out_specs=pl.BlockSpec((None, rows, d), lambda bi, i, *_: (bi, i, 0)),
        scratch_shapes=[
            pltpu.VMEM((N_KV_HEADS, rows + BLOCK, LANES), bf16),
            pltpu.VMEM((N_KV_HEADS, rows + BLOCK, LANES), bf16),
            pltpu.VMEM((POOL_HALO, POOL_WIDTH), jnp.float32),
        ],
    )
    return pl.pallas_call(
        _mixer_kernel,
        grid_spec=grid_spec,
        out_shape=jax.ShapeDtypeStruct(x.shape, x.dtype),
        compiler_params=pltpu.CompilerParams(
            dimension_semantics=("arbitrary", "arbitrary"),
            vmem_limit_bytes=VMEM_LIMIT_BYTES),
        name="mixer",
    )(sinks.astype(jnp.float32), x, g_mix.reshape(1, d), w_in.astype(bf16), b_in.reshape(1, IN_WIDTH),
      w_pool.astype(bf16), b_pool.reshape(1, POOL_WIDTH), pool_scale.reshape(1, POOL_WIDTH),
      w_out.astype(bf16), b_out.reshape(1, d), cos_t, sin_t)


def _ffn(x, g_ffn, w_gate, w_up, w_down, g_final, *, final_norm):
    b, s, d = x.shape
    rows = FFN_ROWS
    tokens = b * s
    assert tokens % rows == 0
    d_ff = w_gate.shape[-1]
    bf16 = jnp.bfloat16
    out = pl.pallas_call(
        functools.partial(_ffn_kernel, final_norm=final_norm),
        grid=(tokens // rows,),
        in_specs=[
            pl.BlockSpec((rows, d), lambda i: (i, 0)),
            _resident((1, d)),
            _resident((d, d_ff)),
            _resident((d, d_ff)),
            _resident((d_ff, d)),
            _resident((1, d)),
        ],
        out_specs=pl.BlockSpec((rows, d), lambda i: (i, 0)),
        out_shape=jax.ShapeDtypeStruct((tokens, d), x.dtype),
        compiler_params=pltpu.CompilerParams(
            dimension_semantics=("arbitrary",),
            vmem_limit_bytes=VMEM_LIMIT_BYTES),
        name="ffn",
    )(x.reshape(tokens, d), g_ffn.reshape(1, d), w_gate.astype(bf16), w_up.astype(bf16),
      w_down.astype(bf16), g_final.reshape(1, d))
    return out.reshape(b, s, d)


def kernel(x, g_mix, w_in, b_in, sinks, w_pool, b_pool, pool_scale, w_out, b_out, g_ffn, w_gate, w_up, w_down, g_final):
    depth = g_mix.shape[0]
    for layer in range(depth):
        x = _mixer(x, g_mix[layer], w_in[layer], b_in[layer], sinks[layer], w_pool[layer],
                   b_pool[layer], pool_scale[layer], w_out[layer], b_out[layer])
        x = _ffn(x, g_ffn[layer], w_gate[layer], w_up[layer], w_down[layer], g_final,
                 final_norm=layer == depth - 1)
    return x
```

```python
import functools

import jax
import jax.numpy as jnp
from jax import lax
from jax.experimental import pallas as pl
from jax.experimental.pallas import tpu as pltpu

D_MODEL = 1024
HEAD_DIM = 64
HALF_HEAD = HEAD_DIM // 2
N_Q_HEADS = 8
N_KV_HEADS = 2
GQA_GROUP = N_Q_HEADS // N_KV_HEADS
ATTN_WIDTH = N_Q_HEADS * HEAD_DIM
KV_WIDTH = N_KV_HEADS * HEAD_DIM
POOL_SIZES = (2, 4, 8, 16)
N_POOL_GROUPS = len(POOL_SIZES)
POOL_GROUP_WIDTH = 128
POOL_WIDTH = N_POOL_GROUPS * POOL_GROUP_WIDTH
IN_WIDTH = ATTN_WIDTH + 2 * KV_WIDTH + POOL_WIDTH
WINDOW = 128
BLOCK = 128
ROPE_THETA = 10000.0
RMS_EPS = 1e-5
POOL_HALO = 16

LANES = 128
MXU_WIDTH = 256
VMEM_LIMIT_BYTES = 60 * 1024 * 1024

ROWS = 512


def _rmsnorm(x, g):
    ms = jnp.mean(x * x, axis=-1, keepdims=True)
    return x * lax.rsqrt(ms + RMS_EPS) * g


def _ffn_block(x, g_ref, wg_ref, wu_ref, wd_ref, gf_ref, final_norm):
    h = _rmsnorm(x, g_ref[...]).astype(jnp.bfloat16)
    gate = jnp.dot(h, wg_ref[...], preferred_element_type=jnp.float32)
    up = jnp.dot(h, wu_ref[...], preferred_element_type=jnp.float32)
    act = (gate * jax.nn.sigmoid(gate) * up).astype(jnp.bfloat16)
    y = x + jnp.dot(act, wd_ref[...], preferred_element_type=jnp.float32)
    return _rmsnorm(y, gf_ref[...]) if final_norm else y


def _attention_block(j, seq_block, sinks_ref, q_ref, kext_ref, vext_ref):
    rows = q_ref.shape[0]
    qi = lax.broadcasted_iota(jnp.int32, (BLOCK, 2 * BLOCK), 0) + BLOCK
    sj = lax.broadcasted_iota(jnp.int32, (BLOCK, 2 * BLOCK), 1)
    delta = qi - sj
    key_pos = seq_block * rows + j * BLOCK + sj - BLOCK
    visible = (delta >= 0) & (delta < WINDOW) & (key_pos >= 0)
    bias = jnp.where(visible, 0.0, -jnp.inf).astype(jnp.float32)
    low_b = lax.broadcasted_iota(jnp.int32, (BLOCK, LANES), 1) < HEAD_DIM
    pairs_per_group = GQA_GROUP // 2

    pair_outs = []
    for g in range(N_KV_HEADS):
        kband = kext_ref[g, j * BLOCK:(j + 2) * BLOCK, :]
        vband = vext_ref[g, j * BLOCK:(j + 2) * BLOCK, :]
        qm = []
        for p in range(pairs_per_group):
            c = (g * pairs_per_group + p) * LANES
            qp = q_ref[j * BLOCK:(j + 1) * BLOCK, c:c + LANES]
            qm.append(jnp.where(low_b, qp, jnp.zeros_like(qp)))
            qm.append(jnp.where(low_b, jnp.zeros_like(qp), qp))
        qm = jnp.concatenate(qm, axis=0)
        s = lax.dot_general(qm, kband, (((1,), (1,)), ((), ())),
                            preferred_element_type=jnp.float32)
        ps, inv_ls = [], []
        for hh in range(GQA_GROUP):
            sink = sinks_ref[g * GQA_GROUP + hh]
            s_h = s[hh * BLOCK:(hh + 1) * BLOCK] + bias
            m = jnp.maximum(jnp.max(s_h, axis=-1, keepdims=True), sink)
            p_h = jnp.exp(s_h - m)
            l = jnp.sum(p_h, axis=-1, keepdims=True) + jnp.exp(sink - m)
            ps.append(p_h.astype(jnp.bfloat16))
            inv_ls.append(1.0 / l)
        pm = jnp.concatenate(ps, axis=0)
        o = jnp.dot(pm, vband, preferred_element_type=jnp.float32)
        for p in range(pairs_per_group):
            o_a = o[(2 * p) * BLOCK:(2 * p + 1) * BLOCK] * inv_ls[2 * p]
            o_b = o[(2 * p + 1) * BLOCK:(2 * p + 2) * BLOCK] * inv_ls[2 * p + 1]
            pair_outs.append(jnp.where(low_b, o_a, o_b))
    return jnp.concatenate(pair_outs, axis=1)


def _store_projection(z, seq_block, wpool_ref, bpool_ref, pscale_ref, cos_ref, sin_ref,
                      q_ref, kext_ref, vext_ref, utail_ref, pool_ref):
    rows = z[0].shape[0]
    lane = lax.broadcasted_iota(jnp.int32, (rows, LANES), 1)
    low_half = lane < HEAD_DIM
    first_half = (lane % HEAD_DIM) < HALF_HEAD
    cos = cos_ref[...]
    sin = sin_ref[...]

    def rope(t):
        partner = jnp.where(first_half, pltpu.roll(t, LANES - HALF_HEAD, 1), pltpu.roll(t, HALF_HEAD, 1))
        return t * cos + partner * sin

    q_pieces = ATTN_WIDTH // LANES
    scale = HEAD_DIM ** -0.5
    for p in range(q_pieces):
        q_ref[:, p * LANES:(p + 1) * LANES] = (rope(z[p]) * scale).astype(q_ref.dtype)

    keep = seq_block > 0
    for ref in (kext_ref, vext_ref):
        tail = ref[:, rows:rows + BLOCK, :]
        ref[:, 0:BLOCK, :] = jnp.where(keep, tail, jnp.zeros_like(tail))

    k = rope(z[q_pieces])
    v = z[q_pieces + 1]
    k_sw = pltpu.roll(k, HEAD_DIM, 1)
    v_sw = pltpu.roll(v, HEAD_DIM, 1)
    kext_ref[0, BLOCK:, :] = jnp.where(low_half, k, k_sw).astype(kext_ref.dtype)
    kext_ref[1, BLOCK:, :] = jnp.where(low_half, k_sw, k).astype(kext_ref.dtype)
    vext_ref[0, BLOCK:, :] = jnp.where(low_half, v, v_sw).astype(vext_ref.dtype)
    vext_ref[1, BLOCK:, :] = jnp.where(low_half, v_sw, v).astype(vext_ref.dtype)

    halo = utail_ref[...]
    halo = jnp.where(keep, halo, jnp.zeros_like(halo))
    pos = seq_block * rows + lax.broadcasted_iota(jnp.int32, (rows, LANES), 0)
    for gi, size in enumerate(POOL_SIZES):
        cols = slice(gi * LANES, (gi + 1) * LANES)
        u_g = z[q_pieces + 2 + gi]
        utail_ref[:, cols] = u_g[rows - POOL_HALO:]
        acc = jnp.concatenate([halo[:, cols], u_g], axis=0)
        span = 1
        while span < size:
            acc = acc + pltpu.roll(acc, span, 0)
            span *= 2
        count = jnp.minimum(pos + 1, size).astype(jnp.float32)
        mixed = acc[POOL_HALO:] / count - u_g
        y = jnp.dot(mixed.astype(jnp.bfloat16), wpool_ref[gi], preferred_element_type=jnp.float32)
        pool_ref[:, cols] = ((y + bpool_ref[:, cols]) * pscale_ref[:, cols]).astype(pool_ref.dtype)


def _layer_kernel(sinks_ref, xprev_ref, xnext_ref, gmix_ref, win_ref, bin_ref, wpool_ref, bpool_ref,
                  pscale_ref, wout_ref, bout_ref, cos_ref, sin_ref, gffn_ref, wg_ref, wu_ref, wd_ref, gf_ref,
                  o_ref, h_ref, q_ref, kext_ref, vext_ref, utail_ref, pool_ref, *, seq_blocks, final_norm):
    i = pl.program_id(0)
    rows = o_ref.shape[0]

    @pl.when(i == 0)
    def _():
        q_ref[...] = jnp.zeros(q_ref.shape, q_ref.dtype)
        kext_ref[...] = jnp.zeros(kext_ref.shape, kext_ref.dtype)
        vext_ref[...] = jnp.zeros(vext_ref.shape, vext_ref.dtype)
        utail_ref[...] = jnp.zeros(utail_ref.shape, utail_ref.dtype)
        pool_ref[...] = jnp.zeros(pool_ref.shape, pool_ref.dtype)
        h_ref[...] = _rmsnorm(xprev_ref[...], gmix_ref[...]).astype(h_ref.dtype)

    prev_seq_block = (i + seq_blocks - 1) % seq_blocks
    h = h_ref[...]
    n_chunks = IN_WIDTH // MXU_WIDTH
    n_attn = rows // BLOCK
    z, attn = [], []
    for t in range(max(n_chunks, n_attn)):
        if t < n_attn:
            attn.append(_attention_block(t, prev_seq_block, sinks_ref, q_ref, kext_ref, vext_ref))
        if t < n_chunks:
            cols = slice(t * MXU_WIDTH, (t + 1) * MXU_WIDTH)
            zc = jnp.dot(h, win_ref[:, cols], preferred_element_type=jnp.float32) + bin_ref[:, cols]
            z += [zc[:, c * LANES:(c + 1) * LANES] for c in range(MXU_WIDTH // LANES)]

    cat = jnp.concatenate([jnp.concatenate(attn, axis=0).astype(jnp.bfloat16), pool_ref[...]], axis=1)
    x1 = xprev_ref[...] + jnp.dot(cat, wout_ref[...], preferred_element_type=jnp.float32) + bout_ref[...]

    _store_projection(z, i % seq_blocks, wpool_ref, bpool_ref, pscale_ref, cos_ref, sin_ref,
                      q_ref, kext_ref, vext_ref, utail_ref, pool_ref)

    h_ref[...] = _rmsnorm(xnext_ref[...], gmix_ref[...]).astype(h_ref.dtype)

    o_ref[...] = _ffn_block(x1, gffn_ref, wg_ref, wu_ref, wd_ref, gf_ref, final_norm)


def _resident(shape):
    zeros = (0,) * len(shape)
    return pl.BlockSpec(shape, lambda *_: zeros, pipeline_mode=pl.Buffered(1))


def _rope_tables(seq):
    inv_freq = 1.0 / (ROPE_THETA ** (jnp.arange(0, HEAD_DIM, 2, dtype=jnp.float32) / HEAD_DIM))
    ang = jnp.arange(seq, dtype=jnp.float32)[:, None] * inv_freq[None, :]
    cos, sin = jnp.cos(ang), jnp.sin(ang)
    return jnp.tile(cos, (1, 4)), jnp.concatenate([-sin, sin, -sin, sin], axis=1)


def _layer(x, g_mix, w_in, b_in, sinks, w_pool, b_pool, pool_scale, w_out, b_out,
           g_ffn, w_gate, w_up, w_down, g_final, *, final_norm):
    b, s, d = x.shape
    rows = ROWS
    assert s % rows == 0 and rows % BLOCK == 0 and d == D_MODEL and IN_WIDTH % MXU_WIDTH == 0
    seq_blocks = s // rows
    n = b * seq_blocks
    d_ff = w_gate.shape[-1]
    cos_t, sin_t = _rope_tables(s)
    bf16 = jnp.bfloat16

    def prev(i, *_):
        return (jnp.maximum(i - 1, 0), 0)

    def cur_pos(i, *_):
        return (jnp.minimum(i, n - 1) % seq_blocks, 0)

    def nxt(i, *_):
        return (jnp.minimum(i + 1, n - 1), 0)

    x2d = x.reshape(b * s, d)
    grid_spec = pltpu.PrefetchScalarGridSpec(
        num_scalar_prefetch=1,
        grid=(n + 1,),
        in_specs=[
            pl.BlockSpec((rows, d), prev),
            pl.BlockSpec((rows, d), nxt),
            _resident((1, d)),
            _resident((d, IN_WIDTH)),
            _resident((1, IN_WIDTH)),
            _resident((N_POOL_GROUPS, POOL_GROUP_WIDTH, POOL_GROUP_WIDTH)),
            _resident((1, POOL_WIDTH)),
            _resident((1, POOL_WIDTH)),
            _resident((ATTN_WIDTH + POOL_WIDTH, d)),
            _resident((1, d)),
            pl.BlockSpec((rows, LANES), cur_pos),
            pl.BlockSpec((rows, LANES), cur_pos),
            _resident((1, d)),
            _resident((d, d_ff)),
            _resident((d, d_ff)),
            _resident((d_ff, d)),
            _resident((1, d)),
        ],
        out_specs=pl.BlockSpec((rows, d), prev),
        scratch_shapes=[
            pltpu.VMEM((rows, d), bf16),
            pltpu.VMEM((rows, ATTN_WIDTH), bf16),
            pltpu.VMEM((N_KV_HEADS, rows + BLOCK, LANES), bf16),
            pltpu.VMEM((N_KV_HEADS, rows + BLOCK, LANES), bf16),
            pltpu.VMEM((POOL_HALO, POOL_WIDTH), jnp.float32),
            pltpu.VMEM((rows, POOL_WIDTH), bf16),
        ],
    )
    out = pl.pallas_call(
        functools.partial(_layer_kernel, seq_blocks=seq_blocks, final_norm=final_norm),
        grid_spec=grid_spec,
        out_shape=jax.ShapeDtypeStruct((b * s, d), x.dtype),
        compiler_params=pltpu.CompilerParams(
            dimension_semantics=("arbitrary",),
            vmem_limit_bytes=VMEM_LIMIT_BYTES),
        name="layer",
    )(sinks.astype(jnp.float32), x2d, x2d, g_mix.reshape(1, d), w_in.astype(bf16),
      b_in.reshape(1, IN_WIDTH), w_pool.astype(bf16), b_pool.reshape(1, POOL_WIDTH),
      pool_scale.reshape(1, POOL_WIDTH), w_out.astype(bf16), b_out.reshape(1, d), cos_t, sin_t,
      g_ffn.reshape(1, d), w_gate.astype(bf16), w_up.astype(bf16), w_down.astype(bf16),
      g_final.reshape(1, d))
    return out.reshape(b, s, d)


def kernel(x, g_mix, w_in, b_in, sinks, w_pool, b_pool, pool_scale, w_out, b_out, g_ffn, w_gate, w_up, w_down, g_final):
    depth = g_mix.shape[0]
    for layer in range(depth):
        x = _layer(x, g_mix[layer], w_in[layer], b_in[layer], sinks[layer], w_pool[layer],
                   b_pool[layer], pool_scale[layer], w_out[layer], b_out[layer],
                   g_ffn[layer], w_gate[layer], w_up[layer], w_down[layer], g_final,
                   final_norm=layer == depth - 1)
    return x
```

```python
import functools

import jax
import jax.numpy as jnp
from jax import lax
from jax.experimental import pallas as pl
from jax.experimental.pallas import tpu as pltpu

D_MODEL = 1024
HEAD_DIM = 64
HALF_HEAD = HEAD_DIM // 2
N_Q_HEADS = 8
N_KV_HEADS = 2
GQA_GROUP = N_Q_HEADS // N_KV_HEADS
ATTN_WIDTH = N_Q_HEADS * HEAD_DIM
KV_WIDTH = N_KV_HEADS * HEAD_DIM
POOL_SIZES = (2, 4, 8, 16)
N_POOL_GROUPS = len(POOL_SIZES)
POOL_GROUP_WIDTH = 128
POOL_WIDTH = N_POOL_GROUPS * POOL_GROUP_WIDTH
IN_WIDTH = ATTN_WIDTH + 2 * KV_WIDTH + POOL_WIDTH
WINDOW = 128
BLOCK = 128
ROPE_THETA = 10000.0
RMS_EPS = 1e-5
POOL_HALO = 16

LANES = 128
MXU_WIDTH = 256
VMEM_LIMIT_BYTES = 60 * 1024 * 1024

BF16_SUBLANES = 16
ROWS = 512
WEIGHT_CHUNKS = 16


def _rmsnorm(x, g):
    ms = jnp.mean(x * x, axis=-1, keepdims=True)
    return x * lax.rsqrt(ms + RMS_EPS) * g


def _ffn_block(x, g_ref, wg_ref, wu_ref, wd_ref, gf_ref, final_norm):
    h = _rmsnorm(x, g_ref[...]).astype(jnp.bfloat16)
    gate = jnp.dot(h, wg_ref[...], preferred_element_type=jnp.float32)
    up = jnp.dot(h, wu_ref[...], preferred_element_type=jnp.float32)
    act = (gate * jax.nn.sigmoid(gate) * up).astype(jnp.bfloat16)
    y = x + jnp.dot(act, wd_ref[...], preferred_element_type=jnp.float32)
    return _rmsnorm(y, gf_ref[...]) if final_norm else y


def _attention_block(j, seq_block, sinks_ref, q_ref, kext_ref, vext_ref):
    rows = q_ref.shape[0]
    qi = lax.broadcasted_iota(jnp.int32, (BLOCK, 2 * BLOCK), 0) + BLOCK
    sj = lax.broadcasted_iota(jnp.int32, (BLOCK, 2 * BLOCK), 1)
    delta = qi - sj
    key_pos = seq_block * rows + j * BLOCK + sj - BLOCK
    visible = (delta >= 0) & (delta < WINDOW) & (key_pos >= 0)
    bias = jnp.where(visible, 0.0, -jnp.inf).astype(jnp.float32)
    low_b = lax.broadcasted_iota(jnp.int32, (BLOCK, LANES), 1) < HEAD_DIM
    pairs_per_group = GQA_GROUP // 2

    pair_outs = []
    for g in range(N_KV_HEADS):
        kband = kext_ref[g, j * BLOCK:(j + 2) * BLOCK, :]
        vband = vext_ref[g, j * BLOCK:(j + 2) * BLOCK, :]
        qm = []
        for p in range(pairs_per_group):
            c = (g * pairs_per_group + p) * LANES
            qp = q_ref[j * BLOCK:(j + 1) * BLOCK, c:c + LANES]
            qm.append(jnp.where(low_b, qp, jnp.zeros_like(qp)))
            qm.append(jnp.where(low_b, jnp.zeros_like(qp), qp))
        qm = jnp.concatenate(qm, axis=0)
        s = lax.dot_general(qm, kband, (((1,), (1,)), ((), ())),
                            preferred_element_type=jnp.float32)
        ps, inv_ls = [], []
        for hh in range(GQA_GROUP):
            sink = sinks_ref[g * GQA_GROUP + hh]
            s_h = s[hh * BLOCK:(hh + 1) * BLOCK] + bias
            m = jnp.maximum(jnp.max(s_h, axis=-1, keepdims=True), sink)
            p_h = jnp.exp(s_h - m)
            l = jnp.sum(p_h, axis=-1, keepdims=True) + jnp.exp(sink - m)
            ps.append(p_h.astype(jnp.bfloat16))
            inv_ls.append(1.0 / l)
        pm = jnp.concatenate(ps, axis=0)
        o = jnp.dot(pm, vband, preferred_element_type=jnp.float32)
        for p in range(pairs_per_group):
            o_a = o[(2 * p) * BLOCK:(2 * p + 1) * BLOCK] * inv_ls[2 * p]
            o_b = o[(2 * p + 1) * BLOCK:(2 * p + 2) * BLOCK] * inv_ls[2 * p + 1]
            pair_outs.append(jnp.where(low_b, o_a, o_b))
    return jnp.concatenate(pair_outs, axis=1)


def _store_projection(z, seq_block, wpool_ref, bpool_ref, pscale_ref, cos_ref, sin_ref,
                      q_ref, kext_ref, vext_ref, utail_ref, pool_ref):
    rows = z[0].shape[0]
    lane = lax.broadcasted_iota(jnp.int32, (rows, LANES), 1)
    low_half = lane < HEAD_DIM
    first_half = (lane % HEAD_DIM) < HALF_HEAD
    cos = cos_ref[...]
    sin = sin_ref[...]

    def rope(t):
        partner = jnp.where(first_half, pltpu.roll(t, LANES - HALF_HEAD, 1), pltpu.roll(t, HALF_HEAD, 1))
        return t * cos + partner * sin

    q_pieces = ATTN_WIDTH // LANES
    scale = HEAD_DIM ** -0.5
    for p in range(q_pieces):
        q_ref[:, p * LANES:(p + 1) * LANES] = (rope(z[p]) * scale).astype(q_ref.dtype)

    keep = seq_block > 0
    for ref in (kext_ref, vext_ref):
        tail = ref[:, rows:rows + BLOCK, :]
        ref[:, 0:BLOCK, :] = jnp.where(keep, tail, jnp.zeros_like(tail))

    k = rope(z[q_pieces])
    v = z[q_pieces + 1]
    k_sw = pltpu.roll(k, HEAD_DIM, 1)
    v_sw = pltpu.roll(v, HEAD_DIM, 1)
    kext_ref[0, BLOCK:, :] = jnp.where(low_half, k, k_sw).astype(kext_ref.dtype)
    kext_ref[1, BLOCK:, :] = jnp.where(low_half, k_sw, k).astype(kext_ref.dtype)
    vext_ref[0, BLOCK:, :] = jnp.where(low_half, v, v_sw).astype(vext_ref.dtype)
    vext_ref[1, BLOCK:, :] = jnp.where(low_half, v_sw, v).astype(vext_ref.dtype)

    halo = utail_ref[...]
    halo = jnp.where(keep, halo, jnp.zeros_like(halo))
    pos = seq_block * rows + lax.broadcasted_iota(jnp.int32, (rows, LANES), 0)
    for gi, size in enumerate(POOL_SIZES):
        cols = slice(gi * LANES, (gi + 1) * LANES)
        u_g = z[q_pieces + 2 + gi]
        utail_ref[:, cols] = u_g[rows - POOL_HALO:]
        acc = jnp.concatenate([halo[:, cols], u_g], axis=0)
        span = 1
        while span < size:
            acc = acc + pltpu.roll(acc, span, 0)
            span *= 2
        count = jnp.minimum(pos + 1, size).astype(jnp.float32)
        mixed = acc[POOL_HALO:] / count - u_g
        y = jnp.dot(mixed.astype(jnp.bfloat16), wpool_ref[gi].astype(jnp.bfloat16),
                    preferred_element_type=jnp.float32)
        pool_ref[:, cols] = ((y + bpool_ref[:, cols]) * pscale_ref[:, cols]).astype(pool_ref.dtype)


def _layer_step(i, sinks_ref, xprev_ref, xnext_ref, gmix_ref, bin_ref, wpool_ref, bpool_ref, pscale_ref,
                bout_ref, cos_ref, sin_ref, gffn_ref, gf_ref, o_ref, win_ref, wout_ref, wg_ref, wu_ref, wd_ref,
                h_ref, q_ref, kext_ref, vext_ref, utail_ref, pool_ref, *, seq_blocks, final_norm):
    rows = o_ref.shape[0]

    @pl.when(i == 0)
    def _():
        q_ref[...] = jnp.zeros(q_ref.shape, q_ref.dtype)
        kext_ref[...] = jnp.zeros(kext_ref.shape, kext_ref.dtype)
        vext_ref[...] = jnp.zeros(vext_ref.shape, vext_ref.dtype)
        utail_ref[...] = jnp.zeros(utail_ref.shape, utail_ref.dtype)
        pool_ref[...] = jnp.zeros(pool_ref.shape, pool_ref.dtype)
        h_ref[...] = _rmsnorm(xprev_ref[...], gmix_ref[...]).astype(h_ref.dtype)

    prev_seq_block = (i + seq_blocks - 1) % seq_blocks
    h = h_ref[...]
    n_chunks = IN_WIDTH // MXU_WIDTH
    n_attn = rows // BLOCK
    z, attn = [], []
    for t in range(max(n_chunks, n_attn)):
        if t < n_attn:
            attn.append(_attention_block(t, prev_seq_block, sinks_ref, q_ref, kext_ref, vext_ref))
        if t < n_chunks:
            cols = slice(t * MXU_WIDTH, (t + 1) * MXU_WIDTH)
            zc = jnp.dot(h, win_ref[:, cols], preferred_element_type=jnp.float32) + bin_ref[:, cols]
            z += [zc[:, c * LANES:(c + 1) * LANES] for c in range(MXU_WIDTH // LANES)]

    cat = jnp.concatenate([jnp.concatenate(attn, axis=0).astype(jnp.bfloat16), pool_ref[...]], axis=1)
    x1 = xprev_ref[...] + jnp.dot(cat, wout_ref[...], preferred_element_type=jnp.float32) + bout_ref[...]

    _store_projection(z, i % seq_blocks, wpool_ref, bpool_ref, pscale_ref, cos_ref, sin_ref,
                      q_ref, kext_ref, vext_ref, utail_ref, pool_ref)

    h_ref[...] = _rmsnorm(xnext_ref[...], gmix_ref[...]).astype(h_ref.dtype)

    o_ref[...] = _ffn_block(x1, gffn_ref, wg_ref, wu_ref, wd_ref, gf_ref, final_norm)


def _layer_kernel(sinks_ref, xprev_ref, xnext_ref, gmix_ref, win32_ref, bin_ref, wpool_ref, bpool_ref,
                  pscale_ref, wout32_ref, bout_ref, cos_ref, sin_ref, gffn_ref, wg32_ref, wu32_ref, wd32_ref,
                  gf_ref, o_ref, win_ref, wout_ref, wg_ref, wu_ref, wd_ref, *scratch, w_chunks, **static):
    i = pl.program_id(0)

    @pl.when(i < w_chunks)
    def _():
        for src, dst in ((win32_ref, win_ref), (wout32_ref, wout_ref), (wg32_ref, wg_ref),
                         (wu32_ref, wu_ref), (wd32_ref, wd_ref)):
            chunk = src.shape[0]
            dst[pl.ds(pl.multiple_of(i * chunk, chunk), chunk), :] = src[...].astype(dst.dtype)

    @pl.when(i >= w_chunks)
    def _():
        _layer_step(i - w_chunks, sinks_ref, xprev_ref, xnext_ref, gmix_ref, bin_ref, wpool_ref, bpool_ref,
                    pscale_ref, bout_ref, cos_ref, sin_ref, gffn_ref, gf_ref, o_ref, win_ref, wout_ref,
                    wg_ref, wu_ref, wd_ref, *scratch, **static)


def _resident(shape):
    zeros = (0,) * len(shape)
    return pl.BlockSpec(shape, lambda *_: zeros, pipeline_mode=pl.Buffered(1))


def _rope_tables(seq):
    inv_freq = 1.0 / (ROPE_THETA ** (jnp.arange(0, HEAD_DIM, 2, dtype=jnp.float32) / HEAD_DIM))
    ang = jnp.arange(seq, dtype=jnp.float32)[:, None] * inv_freq[None, :]
    cos, sin = jnp.cos(ang), jnp.sin(ang)
    return jnp.tile(cos, (1, 4)), jnp.concatenate([-sin, sin, -sin, sin], axis=1)


def _layer(x, g_mix, w_in, b_in, sinks, w_pool, b_pool, pool_scale, w_out, b_out,
           g_ffn, w_gate, w_up, w_down, g_final, *, final_norm):
    b, s, d = x.shape
    rows = ROWS
    assert s % rows == 0 and rows % BLOCK == 0 and d == D_MODEL and IN_WIDTH % MXU_WIDTH == 0
    seq_blocks = s // rows
    n = b * seq_blocks
    d_ff = w_gate.shape[-1]
    cos_t, sin_t = _rope_tables(s)
    bf16 = jnp.bfloat16
    lead = WEIGHT_CHUNKS

    def prev(i, *_):
        return (jnp.clip(i - lead - 1, 0, n - 1), 0)

    def cur_pos(i, *_):
        return (jnp.clip(i - lead, 0, n - 1) % seq_blocks, 0)

    def nxt(i, *_):
        return (jnp.clip(i - lead + 1, 0, n - 1), 0)

    def weight_chunk(w):
        chunk = w.shape[0] // lead
        assert chunk * lead == w.shape[0] and chunk % BF16_SUBLANES == 0
        return pl.BlockSpec((chunk, w.shape[1]), lambda i, *_: (jnp.minimum(i, lead - 1), 0))

    x2d = x.reshape(b * s, d)
    weights = (w_in, w_out, w_gate, w_up, w_down)
    grid_spec = pltpu.PrefetchScalarGridSpec(
        num_scalar_prefetch=1,
        grid=(lead + n + 1,),
        in_specs=[
            pl.BlockSpec((rows, d), prev),
            pl.BlockSpec((rows, d), nxt),
            _resident((1, d)),
            weight_chunk(w_in),
            _resident((1, IN_WIDTH)),
            _resident((N_POOL_GROUPS, POOL_GROUP_WIDTH, POOL_GROUP_WIDTH)),
            _resident((1, POOL_WIDTH)),
            _resident((1, POOL_WIDTH)),
            weight_chunk(w_out),
            _resident((1, d)),
            pl.BlockSpec((rows, LANES), cur_pos),
            pl.BlockSpec((rows, LANES), cur_pos),
            _resident((1, d)),
            weight_chunk(w_gate),
            weight_chunk(w_up),
            weight_chunk(w_down),
            _resident((1, d)),
        ],
        out_specs=pl.BlockSpec((rows, d), prev),
        scratch_shapes=[pltpu.VMEM(w.shape, bf16) for w in weights] + [
            pltpu.VMEM((rows, d), bf16),
            pltpu.VMEM((rows, ATTN_WIDTH), bf16),
            pltpu.VMEM((N_KV_HEADS, rows + BLOCK, LANES), bf16),
            pltpu.VMEM((N_KV_HEADS, rows + BLOCK, LANES), bf16),
            pltpu.VMEM((POOL_HALO, POOL_WIDTH), jnp.float32),
            pltpu.VMEM((rows, POOL_WIDTH), bf16),
        ],
    )
    out = pl.pallas_call(
        functools.partial(_layer_kernel, w_chunks=lead, seq_blocks=seq_blocks, final_norm=final_norm),
        grid_spec=grid_spec,
        out_shape=jax.ShapeDtypeStruct((b * s, d), x.dtype),
        compiler_params=pltpu.CompilerParams(
            dimension_semantics=("arbitrary",),
            vmem_limit_bytes=VMEM_LIMIT_BYTES),
        name="layer",
    )(sinks.astype(jnp.float32), x2d, x2d, g_mix.reshape(1, d), w_in,
      b_in.reshape(1, IN_WIDTH), w_pool, b_pool.reshape(1, POOL_WIDTH),
      pool_scale.reshape(1, POOL_WIDTH), w_out, b_out.reshape(1, d), cos_t, sin_t,
      g_ffn.reshape(1, d), w_gate, w_up, w_down, g_final.reshape(1, d))
    return out.reshape(b, s, d)


def kernel(x, g_mix, w_in, b_in, sinks, w_pool, b_pool, pool_scale, w_out, b_out, g_ffn, w_gate, w_up, w_down, g_final):
    depth = g_mix.shape[0]
    for layer in range(depth):
        x = _layer(x, g_mix[layer], w_in[layer], b_in[layer], sinks[layer], w_pool[layer],
                   b_pool[layer], pool_scale[layer], w_out[layer], b_out[layer],
                   g_ffn[layer], w_gate[layer], w_up[layer], w_down[layer], g_final,
                   final_norm=layer == depth - 1)
    return x
```

```python
import functools

import jax
import jax.numpy as jnp
from jax import lax
from jax.experimental import pallas as pl
from jax.experimental.pallas import tpu as pltpu

D_MODEL = 1024
HEAD_DIM = 64
HALF_HEAD = HEAD_DIM // 2
N_Q_HEADS = 8
N_KV_HEADS = 2
GQA_GROUP = N_Q_HEADS // N_KV_HEADS
ATTN_WIDTH = N_Q_HEADS * HEAD_DIM
KV_WIDTH = N_KV_HEADS * HEAD_DIM
POOL_SIZES = (2, 4, 8, 16)
N_POOL_GROUPS = len(POOL_SIZES)
POOL_GROUP_WIDTH = 128
POOL_WIDTH = N_POOL_GROUPS * POOL_GROUP_WIDTH
IN_WIDTH = ATTN_WIDTH + 2 * KV_WIDTH + POOL_WIDTH
WINDOW = 128
BLOCK = 128
ROPE_THETA = 10000.0
RMS_EPS = 1e-5
POOL_HALO = 16

LANES = 128
MXU_WIDTH = 256
VMEM_LIMIT_BYTES = 60 * 1024 * 1024

BF16_SUBLANES = 16
ROWS = 512
WEIGHT_CHUNKS = 16


def _rmsnorm(x, g):
    ms = jnp.mean(x * x, axis=-1, keepdims=True)
    return x * lax.rsqrt(ms + RMS_EPS) * g


def _ffn_block(x, g_ref, wg_ref, wu_ref, wd_ref, gf_ref, final_norm):
    h = _rmsnorm(x, g_ref[...]).astype(jnp.bfloat16)
    gate = jnp.dot(h, wg_ref[...], preferred_element_type=jnp.float32)
    up = jnp.dot(h, wu_ref[...], preferred_element_type=jnp.float32)
    act = (gate * jax.nn.sigmoid(gate) * up).astype(jnp.bfloat16)
    y = x + jnp.dot(act, wd_ref[...], preferred_element_type=jnp.float32)
    return _rmsnorm(y, gf_ref[...]) if final_norm else y


def _attention_block(j, seq_block, sinks_ref, q_ref, kext_ref, vext_ref):
    rows = q_ref.shape[0]
    qi = lax.broadcasted_iota(jnp.int32, (BLOCK, 2 * BLOCK), 0) + BLOCK
    sj = lax.broadcasted_iota(jnp.int32, (BLOCK, 2 * BLOCK), 1)
    delta = qi - sj
    key_pos = seq_block * rows + j * BLOCK + sj - BLOCK
    visible = (delta >= 0) & (delta < WINDOW) & (key_pos >= 0)
    bias = jnp.where(visible, 0.0, -jnp.inf).astype(jnp.float32)
    low_b = lax.broadcasted_iota(jnp.int32, (BLOCK, LANES), 1) < HEAD_DIM
    pairs_per_group = GQA_GROUP // 2

    pair_outs = []
    for g in range(N_KV_HEADS):
        kband = kext_ref[g, j * BLOCK:(j + 2) * BLOCK, :]
        vband = vext_ref[g, j * BLOCK:(j + 2) * BLOCK, :]
        qm = []
        for p in range(pairs_per_group):
            c = (g * pairs_per_group + p) * LANES
            qp = q_ref[j * BLOCK:(j + 1) * BLOCK, c:c + LANES]
            qm.append(jnp.where(low_b, qp, jnp.zeros_like(qp)))
            qm.append(jnp.where(low_b, jnp.zeros_like(qp), qp))
        qm = jnp.concatenate(qm, axis=0)
        s = lax.dot_general(qm, kband, (((1,), (1,)), ((), ())),
                            preferred_element_type=jnp.float32)
        ps, inv_ls = [], []
        for hh in range(GQA_GROUP):
            sink = sinks_ref[g * GQA_GROUP + hh]
            s_h = s[hh * BLOCK:(hh + 1) * BLOCK] + bias
            m = jnp.maximum(jnp.max(s_h, axis=-1, keepdims=True), sink)
            p_h = jnp.exp(s_h - m)
            l = jnp.sum(p_h, axis=-1, keepdims=True) + jnp.exp(sink - m)
            ps.append(p_h.astype(jnp.bfloat16))
            inv_ls.append(1.0 / l)
        pm = jnp.concatenate(ps, axis=0)
        o = jnp.dot(pm, vband, preferred_element_type=jnp.float32)
        for p in range(pairs_per_group):
            o_a = o[(2 * p) * BLOCK:(2 * p + 1) * BLOCK] * inv_ls[2 * p]
            o_b = o[(2 * p + 1) * BLOCK:(2 * p + 2) * BLOCK] * inv_ls[2 * p + 1]
            pair_outs.append(jnp.where(low_b, o_a, o_b))
    return jnp.concatenate(pair_outs, axis=1)


def _store_projection(z, seq_block, wpool_ref, bpool_ref, pscale_ref, rcos_ref, rsin_ref, bcos_ref, bsin_ref,
                      q_ref, kext_ref, vext_ref, utail_ref, pool_ref):
    rows = z[0].shape[0]
    lane = lax.broadcasted_iota(jnp.int32, (rows, LANES), 1)
    low_half = lane < HEAD_DIM
    first_half = (lane % HEAD_DIM) < HALF_HEAD
    rcos, rsin, bcos, bsin = rcos_ref[...], rsin_ref[...], bcos_ref[...], bsin_ref[...]
    cos = rcos * bcos - rsin * bsin
    sin = jnp.where(first_half, -1.0, 1.0) * (rsin * bcos + rcos * bsin)

    def rope(t):
        partner = jnp.where(first_half, pltpu.roll(t, LANES - HALF_HEAD, 1), pltpu.roll(t, HALF_HEAD, 1))
        return t * cos + partner * sin

    q_pieces = ATTN_WIDTH // LANES
    scale = HEAD_DIM ** -0.5
    for p in range(q_pieces):
        q_ref[:, p * LANES:(p + 1) * LANES] = (rope(z[p]) * scale).astype(q_ref.dtype)

    keep = seq_block > 0
    for ref in (kext_ref, vext_ref):
        tail = ref[:, rows:rows + BLOCK, :]
        ref[:, 0:BLOCK, :] = jnp.where(keep, tail, jnp.zeros_like(tail))

    k = rope(z[q_pieces])
    v = z[q_pieces + 1]
    k_sw = pltpu.roll(k, HEAD_DIM, 1)
    v_sw = pltpu.roll(v, HEAD_DIM, 1)
    kext_ref[0, BLOCK:, :] = jnp.where(low_half, k, k_sw).astype(kext_ref.dtype)
    kext_ref[1, BLOCK:, :] = jnp.where(low_half, k_sw, k).astype(kext_ref.dtype)
    vext_ref[0, BLOCK:, :] = jnp.where(low_half, v, v_sw).astype(vext_ref.dtype)
    vext_ref[1, BLOCK:, :] = jnp.where(low_half, v_sw, v).astype(vext_ref.dtype)

    halo = utail_ref[...]
    halo = jnp.where(keep, halo, jnp.zeros_like(halo))
    pos = seq_block * rows + lax.broadcasted_iota(jnp.int32, (rows, LANES), 0)
    for gi, size in enumerate(POOL_SIZES):
        cols = slice(gi * LANES, (gi + 1) * LANES)
        u_g = z[q_pieces + 2 + gi]
        utail_ref[:, cols] = u_g[rows - POOL_HALO:]
        acc = jnp.concatenate([halo[:, cols], u_g], axis=0)
        span = 1
        while span < size:
            acc = acc + pltpu.roll(acc, span, 0)
            span *= 2
        count = jnp.minimum(pos + 1, size).astype(jnp.float32)
        mixed = acc[POOL_HALO:] / count - u_g
        y = jnp.dot(mixed.astype(jnp.bfloat16), wpool_ref[gi].astype(jnp.bfloat16),
                    preferred_element_type=jnp.float32)
        pool_ref[:, cols] = ((y + bpool_ref[:, cols]) * pscale_ref[:, cols]).astype(pool_ref.dtype)


def _layer_step(i, sinks_ref, xprev_ref, xnext_ref, gmix_ref, bin_ref, wpool_ref, bpool_ref, pscale_ref,
                bout_ref, rcos_ref, rsin_ref, bcos_ref, bsin_ref, gffn_ref, gf_ref, o_ref, win_ref, wout_ref, wg_ref, wu_ref, wd_ref,
                h_ref, q_ref, kext_ref, vext_ref, utail_ref, pool_ref, *, seq_blocks, final_norm):
    rows = o_ref.shape[0]

    @pl.when(i == 0)
    def _():
        q_ref[...] = jnp.zeros(q_ref.shape, q_ref.dtype)
        kext_ref[...] = jnp.zeros(kext_ref.shape, kext_ref.dtype)
        vext_ref[...] = jnp.zeros(vext_ref.shape, vext_ref.dtype)
        utail_ref[...] = jnp.zeros(utail_ref.shape, utail_ref.dtype)
        pool_ref[...] = jnp.zeros(pool_ref.shape, pool_ref.dtype)
        h_ref[...] = _rmsnorm(xprev_ref[...], gmix_ref[...]).astype(h_ref.dtype)

    prev_seq_block = (i + seq_blocks - 1) % seq_blocks
    h = h_ref[...]
    n_chunks = IN_WIDTH // MXU_WIDTH
    n_attn = rows // BLOCK
    z, attn = [], []
    for t in range(max(n_chunks, n_attn)):
        if t < n_attn:
            attn.append(_attention_block(t, prev_seq_block, sinks_ref, q_ref, kext_ref, vext_ref))
        if t < n_chunks:
            cols = slice(t * MXU_WIDTH, (t + 1) * MXU_WIDTH)
            zc = jnp.dot(h, win_ref[:, cols], preferred_element_type=jnp.float32) + bin_ref[:, cols]
            z += [zc[:, c * LANES:(c + 1) * LANES] for c in range(MXU_WIDTH // LANES)]

    cat = jnp.concatenate([jnp.concatenate(attn, axis=0).astype(jnp.bfloat16), pool_ref[...]], axis=1)
    x1 = xprev_ref[...] + jnp.dot(cat, wout_ref[...], preferred_element_type=jnp.float32) + bout_ref[...]

    _store_projection(z, i % seq_blocks, wpool_ref, bpool_ref, pscale_ref, rcos_ref, rsin_ref, bcos_ref, bsin_ref,
                      q_ref, kext_ref, vext_ref, utail_ref, pool_ref)

    h_ref[...] = _rmsnorm(xnext_ref[...], gmix_ref[...]).astype(h_ref.dtype)

    o_ref[...] = _ffn_block(x1, gffn_ref, wg_ref, wu_ref, wd_ref, gf_ref, final_norm)


def _layer_kernel(sinks_ref, xprev_ref, xnext_ref, gmix_ref, win32_ref, bin_ref, wpool_ref, bpool_ref,
                  pscale_ref, wout32_ref, bout_ref, rcos_ref, rsin_ref, bcos_ref, bsin_ref, gffn_ref, wg32_ref, wu32_ref, wd32_ref,
                  gf_ref, o_ref, win_ref, wout_ref, wg_ref, wu_ref, wd_ref, *scratch, w_chunks, **static):
    i = pl.program_id(0)

    @pl.when(i < w_chunks)
    def _():
        for src, dst in ((win32_ref, win_ref), (wout32_ref, wout_ref), (wg32_ref, wg_ref),
                         (wu32_ref, wu_ref), (wd32_ref, wd_ref)):
            chunk = src.shape[0]
            dst[pl.ds(pl.multiple_of(i * chunk, chunk), chunk), :] = src[...].astype(dst.dtype)

    @pl.when(i >= w_chunks)
    def _():
        _layer_step(i - w_chunks, sinks_ref, xprev_ref, xnext_ref, gmix_ref, bin_ref, wpool_ref, bpool_ref,
                    pscale_ref, bout_ref, rcos_ref, rsin_ref, bcos_ref, bsin_ref, gffn_ref, gf_ref, o_ref, win_ref, wout_ref,
                    wg_ref, wu_ref, wd_ref, *scratch, **static)


def _resident(shape):
    zeros = (0,) * len(shape)
    return pl.BlockSpec(shape, lambda *_: zeros, pipeline_mode=pl.Buffered(1))


def _rope_tables(rows, seq_blocks):
    inv_freq = 1.0 / (ROPE_THETA ** (jnp.arange(0, HEAD_DIM, 2, dtype=jnp.float32) / HEAD_DIM))
    freq = jnp.tile(inv_freq, LANES // HALF_HEAD)[None, :]
    row_ang = jnp.arange(rows, dtype=jnp.float32)[:, None] * freq
    start_ang = (jnp.arange(seq_blocks, dtype=jnp.float32) * rows)[:, None, None] * freq
    return jnp.cos(row_ang), jnp.sin(row_ang), jnp.cos(start_ang), jnp.sin(start_ang)


def _layer(x, g_mix, w_in, b_in, sinks, w_pool, b_pool, pool_scale, w_out, b_out,
           g_ffn, w_gate, w_up, w_down, g_final, *, final_norm):
    b, s, d = x.shape
    rows = ROWS
    assert s % rows == 0 and rows % BLOCK == 0 and d == D_MODEL and IN_WIDTH % MXU_WIDTH == 0
    seq_blocks = s // rows
    n = b * seq_blocks
    row_cos, row_sin, start_cos, start_sin = _rope_tables(rows, seq_blocks)
    bf16 = jnp.bfloat16
    lead = WEIGHT_CHUNKS

    def prev(i, *_):
        return (jnp.clip(i - lead - 1, 0, n - 1), 0)

    def cur_start(i, *_):
        return (jnp.clip(i - lead, 0, n - 1) % seq_blocks, 0, 0)

    def nxt(i, *_):
        return (jnp.clip(i - lead + 1, 0, n - 1), 0)

    def weight_chunk(w):
        chunk = w.shape[0] // lead
        assert chunk * lead == w.shape[0] and chunk % BF16_SUBLANES == 0
        return pl.BlockSpec((chunk, w.shape[1]), lambda i, *_: (jnp.minimum(i, lead - 1), 0))

    x2d = x.reshape(b * s, d)
    weights = (w_in, w_out, w_gate, w_up, w_down)
    grid_spec = pltpu.PrefetchScalarGridSpec(
        num_scalar_prefetch=1,
        grid=(lead + n + 1,),
        in_specs=[
            pl.BlockSpec((rows, d), prev),
            pl.BlockSpec((rows, d), nxt),
            _resident((1, d)),
            weight_chunk(w_in),
            _resident((1, IN_WIDTH)),
            _resident((N_POOL_GROUPS, POOL_GROUP_WIDTH, POOL_GROUP_WIDTH)),
            _resident((1, POOL_WIDTH)),
            _resident((1, POOL_WIDTH)),
            weight_chunk(w_out),
            _resident((1, d)),
            _resident((rows, LANES)),
            _resident((rows, LANES)),
            pl.BlockSpec((None, 1, LANES), cur_start),
            pl.BlockSpec((None, 1, LANES), cur_start),
            _resident((1, d)),
            weight_chunk(w_gate),
            weight_chunk(w_up),
            weight_chunk(w_down),
            _resident((1, d)),
        ],
        out_specs=pl.BlockSpec((rows, d), prev),
        scratch_shapes=[pltpu.VMEM(w.shape, bf16) for w in weights] + [
            pltpu.VMEM((rows, d), bf16),
            pltpu.VMEM((rows, ATTN_WIDTH), bf16),
            pltpu.VMEM((N_KV_HEADS, rows + BLOCK, LANES), bf16),
            pltpu.VMEM((N_KV_HEADS, rows + BLOCK, LANES), bf16),
            pltpu.VMEM((POOL_HALO, POOL_WIDTH), jnp.float32),
            pltpu.VMEM((rows, POOL_WIDTH), bf16),
        ],
    )
    out = pl.pallas_call(
        functools.partial(_layer_kernel, w_chunks=lead, seq_blocks=seq_blocks, final_norm=final_norm),
        grid_spec=grid_spec,
        out_shape=jax.ShapeDtypeStruct((b * s, d), x.dtype),
        compiler_params=pltpu.CompilerParams(
            dimension_semantics=("arbitrary",),
            vmem_limit_bytes=VMEM_LIMIT_BYTES),
        name="layer",
    )(sinks.astype(jnp.float32), x2d, x2d, g_mix.reshape(1, d), w_in,
      b_in.reshape(1, IN_WIDTH), w_pool, b_pool.reshape(1, POOL_WIDTH),
      pool_scale.reshape(1, POOL_WIDTH), w_out, b_out.reshape(1, d),
      row_cos, row_sin, start_cos, start_sin,
      g_ffn.reshape(1, d), w_gate, w_up, w_down, g_final.reshape(1, d))
    return out.reshape(b, s, d)


def kernel(x, g_mix, w_in, b_in, sinks, w_pool, b_pool, pool_scale, w_out, b_out, g_ffn, w_gate, w_up, w_down, g_final):
    depth = g_mix.shape[0]
    for layer in range(depth):
        x = _layer(x, g_mix[layer], w_in[layer], b_in[layer], sinks[layer], w_pool[layer],
                   b_pool[layer], pool_scale[layer], w_out[layer], b_out[layer],
                   g_ffn[layer], w_gate[layer], w_up[layer], w_down[layer], g_final,
                   final_norm=layer == depth - 1)
    return x
```

```python
import functools

import jax
import jax.numpy as jnp
from jax import lax
from jax.experimental import pallas as pl
from jax.experimental.pallas import tpu as pltpu

D_MODEL = 1024
HEAD_DIM = 64
HALF_HEAD = HEAD_DIM // 2
N_Q_HEADS = 8
N_KV_HEADS = 2
GQA_GROUP = N_Q_HEADS // N_KV_HEADS
ATTN_WIDTH = N_Q_HEADS * HEAD_DIM
KV_WIDTH = N_KV_HEADS * HEAD_DIM
POOL_SIZES = (2, 4, 8, 16)
N_POOL_GROUPS = len(POOL_SIZES)
POOL_GROUP_WIDTH = 128
POOL_WIDTH = N_POOL_GROUPS * POOL_GROUP_WIDTH
IN_WIDTH = ATTN_WIDTH + 2 * KV_WIDTH + POOL_WIDTH
WINDOW = 128
BLOCK = 128
ROPE_THETA = 10000.0
RMS_EPS = 1e-5
POOL_HALO = 16

LANES = 128
MXU_WIDTH = 256
VMEM_LIMIT_BYTES = 60 * 1024 * 1024

BF16_SUBLANES = 16
ROWS = 512
WEIGHT_CHUNKS = 16


def _rmsnorm(x, g):
    ms = jnp.mean(x * x, axis=-1, keepdims=True)
    return x * lax.rsqrt(ms + RMS_EPS) * g


def _residual_projection(residual, lhs, w_ref, bias_ref=None):
    chunks, sumsq = [], None
    for c in range(w_ref.shape[1] // MXU_WIDTH):
        cols = slice(c * MXU_WIDTH, (c + 1) * MXU_WIDTH)
        y = residual(cols) + jnp.dot(lhs, w_ref[:, cols], preferred_element_type=jnp.float32)
        if bias_ref is not None:
            y = y + bias_ref[:, cols]
        part = jnp.sum(y * y, axis=-1, keepdims=True)
        sumsq = part if sumsq is None else sumsq + part
        chunks.append(y)
    return chunks, sumsq


def _rmsnorm_chunks(chunks, sumsq, g_ref):
    width = sum(y.shape[1] for y in chunks)
    r = lax.rsqrt(sumsq * (1.0 / width) + RMS_EPS)
    return [y * r * g_ref[:, c * MXU_WIDTH:(c + 1) * MXU_WIDTH] for c, y in enumerate(chunks)]


def _ffn_block(x_chunks, x_sumsq, g_ref, wg_ref, wu_ref, wd_ref, gf_ref, final_norm):
    h = jnp.concatenate(_rmsnorm_chunks(x_chunks, x_sumsq, g_ref), axis=1).astype(jnp.bfloat16)
    gate = jnp.dot(h, wg_ref[...], preferred_element_type=jnp.float32)
    up = jnp.dot(h, wu_ref[...], preferred_element_type=jnp.float32)
    act = (gate * jax.nn.sigmoid(gate) * up).astype(jnp.bfloat16)
    y_chunks, y_sumsq = _residual_projection(lambda cols: x_chunks[cols.start // MXU_WIDTH], act, wd_ref)
    return _rmsnorm_chunks(y_chunks, y_sumsq, gf_ref) if final_norm else y_chunks


def _attention_block(j, seq_block, sinks_ref, q_ref, kext_ref, vext_ref):
    rows = q_ref.shape[0]
    qi = lax.broadcasted_iota(jnp.int32, (BLOCK, 2 * BLOCK), 0) + BLOCK
    sj = lax.broadcasted_iota(jnp.int32, (BLOCK, 2 * BLOCK), 1)
    delta = qi - sj
    key_pos = seq_block * rows + j * BLOCK + sj - BLOCK
    visible = (delta >= 0) & (delta < WINDOW) & (key_pos >= 0)
    bias = jnp.where(visible, 0.0, -jnp.inf).astype(jnp.float32)
    low_b = lax.broadcasted_iota(jnp.int32, (BLOCK, LANES), 1) < HEAD_DIM
    pairs_per_group = GQA_GROUP // 2

    pair_outs = []
    for g in range(N_KV_HEADS):
        kband = kext_ref[g, j * BLOCK:(j + 2) * BLOCK, :]
        vband = vext_ref[g, j * BLOCK:(j + 2) * BLOCK, :]
        qm = []
        for p in range(pairs_per_group):
            c = (g * pairs_per_group + p) * LANES
            qp = q_ref[j * BLOCK:(j + 1) * BLOCK, c:c + LANES]
            qm.append(jnp.where(low_b, qp, jnp.zeros_like(qp)))
            qm.append(jnp.where(low_b, jnp.zeros_like(qp), qp))
        qm = jnp.concatenate(qm, axis=0)
        s = lax.dot_general(qm, kband, (((1,), (1,)), ((), ())),
                            preferred_element_type=jnp.float32)
        ps, inv_ls = [], []
        for hh in range(GQA_GROUP):
            sink = sinks_ref[g * GQA_GROUP + hh]
            s_h = s[hh * BLOCK:(hh + 1) * BLOCK] + bias
            m = jnp.maximum(jnp.max(s_h, axis=-1, keepdims=True), sink)
            p_h = jnp.exp(s_h - m)
            l = jnp.sum(p_h, axis=-1, keepdims=True) + jnp.exp(sink - m)
            ps.append(p_h.astype(jnp.bfloat16))
            inv_ls.append(1.0 / l)
        pm = jnp.concatenate(ps, axis=0)
        o = jnp.dot(pm, vband, preferred_element_type=jnp.float32)
        for p in range(pairs_per_group):
            o_a = o[(2 * p) * BLOCK:(2 * p + 1) * BLOCK] * inv_ls[2 * p]
            o_b = o[(2 * p + 1) * BLOCK:(2 * p + 2) * BLOCK] * inv_ls[2 * p + 1]
            pair_outs.append(jnp.where(low_b, o_a, o_b))
    return jnp.concatenate(pair_outs, axis=1)


def _store_projection(z, seq_block, wpool_ref, bpool_ref, pscale_ref, rcos_ref, rsin_ref, bcos_ref, bsin_ref,
                      q_ref, kext_ref, vext_ref, utail_ref, pool_ref):
    rows = z[0].shape[0]
    lane = lax.broadcasted_iota(jnp.int32, (rows, LANES), 1)
    low_half = lane < HEAD_DIM
    first_half = (lane % HEAD_DIM) < HALF_HEAD
    rcos, rsin, bcos, bsin = rcos_ref[...], rsin_ref[...], bcos_ref[...], bsin_ref[...]
    cos = rcos * bcos - rsin * bsin
    sin = jnp.where(first_half, -1.0, 1.0) * (rsin * bcos + rcos * bsin)

    def rope(t):
        partner = jnp.where(first_half, pltpu.roll(t, LANES - HALF_HEAD, 1), pltpu.roll(t, HALF_HEAD, 1))
        return t * cos + partner * sin

    q_pieces = ATTN_WIDTH // LANES
    scale = HEAD_DIM ** -0.5
    for p in range(q_pieces):
        q_ref[:, p * LANES:(p + 1) * LANES] = (rope(z[p]) * scale).astype(q_ref.dtype)

    keep = seq_block > 0
    for ref in (kext_ref, vext_ref):
        tail = ref[:, rows:rows + BLOCK, :]
        ref[:, 0:BLOCK, :] = jnp.where(keep, tail, jnp.zeros_like(tail))

    k = rope(z[q_pieces])
    v = z[q_pieces + 1]
    k_sw = pltpu.roll(k, HEAD_DIM, 1)
    v_sw = pltpu.roll(v, HEAD_DIM, 1)
    kext_ref[0, BLOCK:, :] = jnp.where(low_half, k, k_sw).astype(kext_ref.dtype)
    kext_ref[1, BLOCK:, :] = jnp.where(low_half, k_sw, k).astype(kext_ref.dtype)
    vext_ref[0, BLOCK:, :] = jnp.where(low_half, v, v_sw).astype(vext_ref.dtype)
    vext_ref[1, BLOCK:, :] = jnp.where(low_half, v_sw, v).astype(vext_ref.dtype)

    halo = utail_ref[...]
    halo = jnp.where(keep, halo, jnp.zeros_like(halo))
    pos = seq_block * rows + lax.broadcasted_iota(jnp.int32, (rows, LANES), 0)
    for gi, size in enumerate(POOL_SIZES):
        cols = slice(gi * LANES, (gi + 1) * LANES)
        u_g = z[q_pieces + 2 + gi]
        utail_ref[:, cols] = u_g[rows - POOL_HALO:]
        acc = jnp.concatenate([halo[:, cols], u_g], axis=0)
        span = 1
        while span < size:
            acc = acc + pltpu.roll(acc, span, 0)
            span *= 2
        count = jnp.minimum(pos + 1, size).astype(jnp.float32)
        mixed = acc[POOL_HALO:] / count - u_g
        y = jnp.dot(mixed.astype(jnp.bfloat16), wpool_ref[gi].astype(jnp.bfloat16),
                    preferred_element_type=jnp.float32)
        pool_ref[:, cols] = ((y + bpool_ref[:, cols]) * pscale_ref[:, cols]).astype(pool_ref.dtype)


def _layer_step(i, sinks_ref, xprev_ref, xnext_ref, gmix_ref, bin_ref, wpool_ref, bpool_ref, pscale_ref,
                bout_ref, rcos_ref, rsin_ref, bcos_ref, bsin_ref, gffn_ref, gf_ref, o_ref, win_ref, wout_ref, wg_ref, wu_ref, wd_ref,
                h_ref, q_ref, kext_ref, vext_ref, utail_ref, pool_ref, *, seq_blocks, final_norm):
    rows = o_ref.shape[0]
    n_chunks = IN_WIDTH // MXU_WIDTH
    n_attn = rows // BLOCK
    project_refs = (wpool_ref, bpool_ref, pscale_ref, rcos_ref, rsin_ref, bcos_ref, bsin_ref,
                    q_ref, kext_ref, vext_ref, utail_ref, pool_ref)

    def in_projection_chunk(h, t):
        cols = slice(t * MXU_WIDTH, (t + 1) * MXU_WIDTH)
        zc = jnp.dot(h, win_ref[:, cols], preferred_element_type=jnp.float32) + bin_ref[:, cols]
        return [zc[:, c * LANES:(c + 1) * LANES] for c in range(MXU_WIDTH // LANES)]

    @pl.when(i == 0)
    def _():
        kext_ref[...] = jnp.zeros(kext_ref.shape, kext_ref.dtype)
        vext_ref[...] = jnp.zeros(vext_ref.shape, vext_ref.dtype)
        utail_ref[...] = jnp.zeros(utail_ref.shape, utail_ref.dtype)
        h = _rmsnorm(xprev_ref[...], gmix_ref[...]).astype(h_ref.dtype)
        z = sum((in_projection_chunk(h, t) for t in range(n_chunks)), [])
        _store_projection(z, 0, *project_refs)
        h_ref[...] = _rmsnorm(xnext_ref[...], gmix_ref[...]).astype(h_ref.dtype)

    @pl.when(i > 0)
    def _():
        prev_seq_block = (i + seq_blocks - 1) % seq_blocks
        h = h_ref[...]
        z, attn = [], []
        for t in range(max(n_chunks, n_attn)):
            if t < n_attn:
                attn.append(_attention_block(t, prev_seq_block, sinks_ref, q_ref, kext_ref, vext_ref))
            if t < n_chunks:
                z += in_projection_chunk(h, t)

        cat = jnp.concatenate([jnp.concatenate(attn, axis=0).astype(jnp.bfloat16), pool_ref[...]], axis=1)
        x1_chunks, x1_sumsq = _residual_projection(lambda cols: xprev_ref[:, cols], cat, wout_ref, bout_ref)

        _store_projection(z, i % seq_blocks, *project_refs)

        h_ref[...] = _rmsnorm(xnext_ref[...], gmix_ref[...]).astype(h_ref.dtype)

        out_chunks = _ffn_block(x1_chunks, x1_sumsq, gffn_ref, wg_ref, wu_ref, wd_ref, gf_ref, final_norm)
        for c, y in enumerate(out_chunks):
            o_ref[:, c * MXU_WIDTH:(c + 1) * MXU_WIDTH] = y


def _layer_kernel(sinks_ref, xprev_ref, xnext_ref, gmix_ref, win32_ref, bin_ref, wpool_ref, bpool_ref,
                  pscale_ref, wout32_ref, bout_ref, rcos_ref, rsin_ref, bcos_ref, bsin_ref, gffn_ref, wg32_ref, wu32_ref, wd32_ref,
                  gf_ref, o_ref, win_ref, wout_ref, wg_ref, wu_ref, wd_ref, *scratch, w_chunks, **static):
    i = pl.program_id(0)

    @pl.when(i < w_chunks)
    def _():
        for src, dst in ((win32_ref, win_ref), (wout32_ref, wout_ref), (wg32_ref, wg_ref),
                         (wu32_ref, wu_ref), (wd32_ref, wd_ref)):
            chunk = src.shape[0]
            dst[pl.ds(pl.multiple_of(i * chunk, chunk), chunk), :] = src[...].astype(dst.dtype)

    @pl.when(i >= w_chunks)
    def _():
        _layer_step(i - w_chunks, sinks_ref, xprev_ref, xnext_ref, gmix_ref, bin_ref, wpool_ref, bpool_ref,
                    pscale_ref, bout_ref, rcos_ref, rsin_ref, bcos_ref, bsin_ref, gffn_ref, gf_ref, o_ref, win_ref, wout_ref,
                    wg_ref, wu_ref, wd_ref, *scratch, **static)


def _resident(shape):
    zeros = (0,) * len(shape)
    return pl.BlockSpec(shape, lambda *_: zeros, pipeline_mode=pl.Buffered(1))


def _rope_tables(rows, seq_blocks):
    inv_freq = 1.0 / (ROPE_THETA ** (jnp.arange(0, HEAD_DIM, 2, dtype=jnp.float32) / HEAD_DIM))
    freq = jnp.tile(inv_freq, LANES // HALF_HEAD)[None, :]
    row_ang = jnp.arange(rows, dtype=jnp.float32)[:, None] * freq
    start_ang = (jnp.arange(seq_blocks, dtype=jnp.float32) * rows)[:, None, None] * freq
    return jnp.cos(row_ang), jnp.sin(row_ang), jnp.cos(start_ang), jnp.sin(start_ang)


def _layer(x, g_mix, w_in, b_in, sinks, w_pool, b_pool, pool_scale, w_out, b_out,
           g_ffn, w_gate, w_up, w_down, g_final, *, final_norm):
    b, s, d = x.shape
    rows = ROWS
    assert s % rows == 0 and rows % BLOCK == 0 and d == D_MODEL and IN_WIDTH % MXU_WIDTH == 0
    seq_blocks = s // rows
    n = b * seq_blocks
    row_cos, row_sin, start_cos, start_sin = _rope_tables(rows, seq_blocks)
    bf16 = jnp.bfloat16
    lead = WEIGHT_CHUNKS

    def prev(i, *_):
        return (jnp.clip(i - lead - 1, 0, n - 1), 0)

    def cur_start(i, *_):
        return (jnp.clip(i - lead, 0, n - 1) % seq_blocks, 0, 0)

    def nxt(i, *_):
        return (jnp.clip(i - lead + 1, 0, n - 1), 0)

    def weight_chunk(w):
        chunk = w.shape[0] // lead
        assert chunk * lead == w.shape[0] and chunk % BF16_SUBLANES == 0
        return pl.BlockSpec((chunk, w.shape[1]), lambda i, *_: (jnp.minimum(i, lead - 1), 0))

    x2d = x.reshape(b * s, d)
    weights = (w_in, w_out, w_gate, w_up, w_down)
    grid_spec = pltpu.PrefetchScalarGridSpec(
        num_scalar_prefetch=1,
        grid=(lead + n + 1,),
        in_specs=[
            pl.BlockSpec((rows, d), prev),
            pl.BlockSpec((rows, d), nxt),
            _resident((1, d)),
            weight_chunk(w_in),
            _resident((1, IN_WIDTH)),
            _resident((N_POOL_GROUPS, POOL_GROUP_WIDTH, POOL_GROUP_WIDTH)),
            _resident((1, POOL_WIDTH)),
            _resident((1, POOL_WIDTH)),
            weight_chunk(w_out),
            _resident((1, d)),
            _resident((rows, LANES)),
            _resident((rows, LANES)),
            pl.BlockSpec((None, 1, LANES), cur_start),
            pl.BlockSpec((None, 1, LANES), cur_start),
            _resident((1, d)),
            weight_chunk(w_gate),
            weight_chunk(w_up),
            weight_chunk(w_down),
            _resident((1, d)),
        ],
        out_specs=pl.BlockSpec((rows, d), prev),
        scratch_shapes=[pltpu.VMEM(w.shape, bf16) for w in weights] + [
            pltpu.VMEM((rows, d), bf16),
            pltpu.VMEM((rows, ATTN_WIDTH), bf16),
            pltpu.VMEM((N_KV_HEADS, rows + BLOCK, LANES), bf16),
            pltpu.VMEM((N_KV_HEADS, rows + BLOCK, LANES), bf16),
            pltpu.VMEM((POOL_HALO, POOL_WIDTH), jnp.float32),
            pltpu.VMEM((rows, POOL_WIDTH), bf16),
        ],
    )
    out = pl.pallas_call(
        functools.partial(_layer_kernel, w_chunks=lead, seq_blocks=seq_blocks, final_norm=final_norm),
        grid_spec=grid_spec,
        out_shape=jax.ShapeDtypeStruct((b * s, d), x.dtype),
        compiler_params=pltpu.CompilerParams(
            dimension_semantics=("arbitrary",),
            vmem_limit_bytes=VMEM_LIMIT_BYTES),
        name="layer",
    )(sinks.astype(jnp.float32), x2d, x2d, g_mix.reshape(1, d), w_in,
      b_in.reshape(1, IN_WIDTH), w_pool, b_pool.reshape(1, POOL_WIDTH),
      pool_scale.reshape(1, POOL_WIDTH), w_out, b_out.reshape(1, d),
      row_cos, row_sin, start_cos, start_sin,
      g_ffn.reshape(1, d), w_gate, w_up, w_down, g_final.reshape(1, d))
    return out.reshape(b, s, d)


def kernel(x, g_mix, w_in, b_in, sinks, w_pool, b_pool, pool_scale, w_out, b_out, g_ffn, w_gate, w_up, w_down, g_final):
    depth = g_mix.shape[0]
    for layer in range(depth):
        x = _layer(x, g_mix[layer], w_in[layer], b_in[layer], sinks[layer], w_pool[layer],
                   b_pool[layer], pool_scale[layer], w_out[layer], b_out[layer],
                   g_ffn[layer], w_gate[layer], w_up[layer], w_down[layer], g_final,
                   final_norm=layer == depth - 1)
    return x
```

```python
import functools

import jax
import jax.numpy as jnp
from jax import lax
from jax.experimental import pallas as pl
from jax.experimental.pallas import tpu as pltpu

D_MODEL = 1024
HEAD_DIM = 64
HALF_HEAD = HEAD_DIM // 2
N_Q_HEADS = 8
N_KV_HEADS = 2
GQA_GROUP = N_Q_HEADS // N_KV_HEADS
ATTN_WIDTH = N_Q_HEADS * HEAD_DIM
KV_WIDTH = N_KV_HEADS * HEAD_DIM
POOL_SIZES = (2, 4, 8, 16)
N_POOL_GROUPS = len(POOL_SIZES)
POOL_GROUP_WIDTH = 128
POOL_WIDTH = N_POOL_GROUPS * POOL_GROUP_WIDTH
IN_WIDTH = ATTN_WIDTH + 2 * KV_WIDTH + POOL_WIDTH
WINDOW = 128
BLOCK = 128
ROPE_THETA = 10000.0
RMS_EPS = 1e-5
LOG2_E = 1.4426950408889634
POOL_HALO = 16

LANES = 128
MXU_WIDTH = 256
IN_CHUNKS = (256, 256, 256, 256, 256)
VMEM_LIMIT_BYTES = 60 * 1024 * 1024

BF16_SUBLANES = 16
ROWS = 512
WEIGHT_CHUNKS = 16


def _rmsnorm(x, g):
    ms = jnp.mean(x * x, axis=-1, keepdims=True)
    return x * lax.rsqrt(ms + RMS_EPS) * g


def _residual_projection(residual, lhs, w_ref, bias_ref=None):
    chunks, sumsq = [], None
    for c in range(w_ref.shape[1] // MXU_WIDTH):
        cols = slice(c * MXU_WIDTH, (c + 1) * MXU_WIDTH)
        y = residual(cols) + jnp.dot(lhs, w_ref[:, cols], preferred_element_type=jnp.float32)
        if bias_ref is not None:
            y = y + bias_ref[:, cols]
        part = jnp.sum(y * y, axis=-1, keepdims=True)
        sumsq = part if sumsq is None else sumsq + part
        chunks.append(y)
    return chunks, sumsq


def _rmsnorm_chunks(chunks, sumsq, g_ref):
    width = sum(y.shape[1] for y in chunks)
    r = lax.rsqrt(sumsq * (1.0 / width) + RMS_EPS)
    return [y * r * g_ref[:, c * MXU_WIDTH:(c + 1) * MXU_WIDTH] for c, y in enumerate(chunks)]


def _ffn_block(x_chunks, x_sumsq, g_ref, wg_ref, wu_ref, wd_ref, gf_ref, final_norm):
    width = sum(y.shape[1] for y in x_chunks)
    xg = jnp.concatenate([y * g_ref[:, c * MXU_WIDTH:(c + 1) * MXU_WIDTH] for c, y in enumerate(x_chunks)],
                         axis=1).astype(jnp.bfloat16)
    r = lax.rsqrt(x_sumsq * (1.0 / width) + RMS_EPS)
    gate = r * jnp.dot(xg, wg_ref[...], preferred_element_type=jnp.float32)
    up = r * jnp.dot(xg, wu_ref[...], preferred_element_type=jnp.float32)
    act = (gate * jax.nn.sigmoid(gate) * up).astype(jnp.bfloat16)
    y_chunks, y_sumsq = _residual_projection(lambda cols: x_chunks[cols.start // MXU_WIDTH], act, wd_ref)
    return _rmsnorm_chunks(y_chunks, y_sumsq, gf_ref) if final_norm else y_chunks


def _attention_block(j, seq_block, sinks_ref, q_ref, kext_ref, vext_ref):
    rows = q_ref.shape[0]
    qi = lax.broadcasted_iota(jnp.int32, (BLOCK, 2 * BLOCK), 0) + BLOCK
    sj = lax.broadcasted_iota(jnp.int32, (BLOCK, 2 * BLOCK), 1)
    delta = qi - sj
    key_pos = seq_block * rows + j * BLOCK + sj - BLOCK
    visible = (delta >= 0) & (delta < WINDOW) & (key_pos >= 0)
    bias = jnp.where(visible, 0.0, -jnp.inf).astype(jnp.float32)
    low_b = lax.broadcasted_iota(jnp.int32, (BLOCK, LANES), 1) < HEAD_DIM
    pairs_per_group = GQA_GROUP // 2

    pair_outs = []
    for g in range(N_KV_HEADS):
        kband = kext_ref[g, j * BLOCK:(j + 2) * BLOCK, :]
        vband = vext_ref[g, j * BLOCK:(j + 2) * BLOCK, :]
        qm = []
        for p in range(pairs_per_group):
            c = (g * pairs_per_group + p) * LANES
            qp = q_ref[j * BLOCK:(j + 1) * BLOCK, c:c + LANES]
            qm.append(jnp.where(low_b, qp, jnp.zeros_like(qp)))
            qm.append(jnp.where(low_b, jnp.zeros_like(qp), qp))
        qm = jnp.concatenate(qm, axis=0)
        s = lax.dot_general(qm, kband, (((1,), (1,)), ((), ())),
                            preferred_element_type=jnp.float32)
        ps, inv_ls = [], []
        for hh in range(GQA_GROUP):
            sink = sinks_ref[g * GQA_GROUP + hh] * LOG2_E
            s_h = s[hh * BLOCK:(hh + 1) * BLOCK] + bias
            m = jnp.maximum(jnp.max(s_h, axis=-1, keepdims=True), sink)
            p_h = jnp.exp2(s_h - m)
            l = jnp.sum(p_h, axis=-1, keepdims=True) + jnp.exp2(sink - m)
            ps.append(p_h.astype(jnp.bfloat16))
            inv_ls.append(1.0 / l)
        pm = jnp.concatenate(ps, axis=0)
        o = jnp.dot(pm, vband, preferred_element_type=jnp.float32)
        for p in range(pairs_per_group):
            o_a = o[(2 * p) * BLOCK:(2 * p + 1) * BLOCK] * inv_ls[2 * p]
            o_b = o[(2 * p + 1) * BLOCK:(2 * p + 2) * BLOCK] * inv_ls[2 * p + 1]
            pair_outs.append(jnp.where(low_b, o_a, o_b))
    return jnp.concatenate(pair_outs, axis=1)


def _store_projection(z, seq_block, wpool_ref, bpool_ref, pscale_ref, rcos_ref, rsin_ref, bcos_ref, bsin_ref,
                      q_ref, kext_ref, vext_ref, utail_ref, pool_ref):
    rows = z[0].shape[0]
    lane = lax.broadcasted_iota(jnp.int32, (rows, LANES), 1)
    low_half = lane < HEAD_DIM
    first_half = (lane % HEAD_DIM) < HALF_HEAD
    rcos, rsin, bcos, bsin = rcos_ref[...], rsin_ref[...], bcos_ref[...], bsin_ref[...]
    cos = rcos * bcos - rsin * bsin
    sin = jnp.where(first_half, -1.0, 1.0) * (rsin * bcos + rcos * bsin)

    def rope(t):
        partner = jnp.where(first_half, pltpu.roll(t, LANES - HALF_HEAD, 1), pltpu.roll(t, HALF_HEAD, 1))
        return t * cos + partner * sin

    q_pieces = ATTN_WIDTH // LANES
    scale = HEAD_DIM ** -0.5 * LOG2_E
    for p in range(q_pieces):
        q_ref[:, p * LANES:(p + 1) * LANES] = (rope(z[p]) * scale).astype(q_ref.dtype)

    keep = seq_block > 0
    for ref in (kext_ref, vext_ref):
        tail = ref[:, rows:rows + BLOCK, :]
        ref[:, 0:BLOCK, :] = jnp.where(keep, tail, jnp.zeros_like(tail))

    k = rope(z[q_pieces])
    v = z[q_pieces + 1]
    k_sw = pltpu.roll(k, HEAD_DIM, 1)
    v_sw = pltpu.roll(v, HEAD_DIM, 1)
    kext_ref[0, BLOCK:, :] = jnp.where(low_half, k, k_sw).astype(kext_ref.dtype)
    kext_ref[1, BLOCK:, :] = jnp.where(low_half, k_sw, k).astype(kext_ref.dtype)
    vext_ref[0, BLOCK:, :] = jnp.where(low_half, v, v_sw).astype(vext_ref.dtype)
    vext_ref[1, BLOCK:, :] = jnp.where(low_half, v_sw, v).astype(vext_ref.dtype)

    halo = utail_ref[...]
    halo = jnp.where(keep, halo, jnp.zeros_like(halo))
    pos = seq_block * rows + lax.broadcasted_iota(jnp.int32, (rows, LANES), 0)
    for gi, size in enumerate(POOL_SIZES):
        cols = slice(gi * LANES, (gi + 1) * LANES)
        u_g = z[q_pieces + 2 + gi]
        utail_ref[:, cols] = u_g[rows - POOL_HALO:]
        acc = jnp.concatenate([halo[:, cols], u_g], axis=0)
        span = 1
        while span < size:
            acc = acc + pltpu.roll(acc, span, 0)
            span *= 2
        count = jnp.minimum(pos + 1, size).astype(jnp.float32)
        mixed = acc[POOL_HALO:] / count - u_g
        y = jnp.dot(mixed.astype(jnp.bfloat16), wpool_ref[gi].astype(jnp.bfloat16),
                    preferred_element_type=jnp.float32)
        pool_ref[:, cols] = ((y + bpool_ref[:, cols]) * pscale_ref[:, cols]).astype(pool_ref.dtype)


def _layer_step(i, sinks_ref, xprev_ref, xnext_ref, gmix_ref, bin_ref, wpool_ref, bpool_ref, pscale_ref,
                bout_ref, rcos_ref, rsin_ref, bcos_ref, bsin_ref, gffn_ref, gf_ref, o_ref, win_ref, wout_ref, wg_ref, wu_ref, wd_ref,
                h_ref, q_ref, kext_ref, vext_ref, utail_ref, pool_ref, *, seq_blocks, final_norm):
    rows = o_ref.shape[0]
    chunk_starts = [sum(IN_CHUNKS[:t]) for t in range(len(IN_CHUNKS))]
    n_chunks = len(IN_CHUNKS)
    n_attn = rows // BLOCK
    project_refs = (wpool_ref, bpool_ref, pscale_ref, rcos_ref, rsin_ref, bcos_ref, bsin_ref,
                    q_ref, kext_ref, vext_ref, utail_ref, pool_ref)

    def in_projection_chunk(h, t):
        cols = slice(chunk_starts[t], chunk_starts[t] + IN_CHUNKS[t])
        zc = jnp.dot(h, win_ref[:, cols], preferred_element_type=jnp.float32) + bin_ref[:, cols]
        return [zc[:, c * LANES:(c + 1) * LANES] for c in range(IN_CHUNKS[t] // LANES)]

    @pl.when(i == 0)
    def _():
        kext_ref[...] = jnp.zeros(kext_ref.shape, kext_ref.dtype)
        vext_ref[...] = jnp.zeros(vext_ref.shape, vext_ref.dtype)
        utail_ref[...] = jnp.zeros(utail_ref.shape, utail_ref.dtype)
        h = _rmsnorm(xprev_ref[...], gmix_ref[...]).astype(h_ref.dtype)
        z = sum((in_projection_chunk(h, t) for t in range(n_chunks)), [])
        _store_projection(z, 0, *project_refs)
        h_ref[...] = _rmsnorm(xnext_ref[...], gmix_ref[...]).astype(h_ref.dtype)

    @pl.when(i > 0)
    def _():
        prev_seq_block = (i + seq_blocks - 1) % seq_blocks
        h = h_ref[...]
        z, attn = [], []
        for t in range(max(n_chunks, n_attn)):
            if t < n_attn:
                attn.append(_attention_block(t, prev_seq_block, sinks_ref, q_ref, kext_ref, vext_ref))
            if t < n_chunks:
                z += in_projection_chunk(h, t)

        cat = jnp.concatenate([jnp.concatenate(attn, axis=0).astype(jnp.bfloat16), pool_ref[...]], axis=1)
        x1_chunks, x1_sumsq = _residual_projection(lambda cols: xprev_ref[:, cols], cat, wout_ref, bout_ref)

        _store_projection(z, i % seq_blocks, *project_refs)

        h_ref[...] = _rmsnorm(xnext_ref[...], gmix_ref[...]).astype(h_ref.dtype)

        out_chunks = _ffn_block(x1_chunks, x1_sumsq, gffn_ref, wg_ref, wu_ref, wd_ref, gf_ref, final_norm)
        for c, y in enumerate(out_chunks):
            o_ref[:, c * MXU_WIDTH:(c + 1) * MXU_WIDTH] = y


def _layer_kernel(sinks_ref, xprev_ref, xnext_ref, gmix_ref, win32_ref, bin_ref, wpool_ref, bpool_ref,
                  pscale_ref, wout32_ref, bout_ref, rcos_ref, rsin_ref, bcos_ref, bsin_ref, gffn_ref, wg32_ref, wu32_ref, wd32_ref,
                  gf_ref, o_ref, win_ref, wout_ref, wg_ref, wu_ref, wd_ref, *scratch, w_chunks, **static):
    i = pl.program_id(0)

    @pl.when(i < w_chunks)
    def _():
        for src, dst in ((win32_ref, win_ref), (wout32_ref, wout_ref), (wg32_ref, wg_ref),
                         (wu32_ref, wu_ref), (wd32_ref, wd_ref)):
            chunk = src.shape[0]
            dst[pl.ds(pl.multiple_of(i * chunk, chunk), chunk), :] = src[...].astype(dst.dtype)

    @pl.when(i >= w_chunks)
    def _():
        _layer_step(i - w_chunks, sinks_ref, xprev_ref, xnext_ref, gmix_ref, bin_ref, wpool_ref, bpool_ref,
                    pscale_ref, bout_ref, rcos_ref, rsin_ref, bcos_ref, bsin_ref, gffn_ref, gf_ref, o_ref, win_ref, wout_ref,
                    wg_ref, wu_ref, wd_ref, *scratch, **static)


def _resident(shape):
    zeros = (0,) * len(shape)
    return pl.BlockSpec(shape, lambda *_: zeros, pipeline_mode=pl.Buffered(1))


def _rope_tables(rows, seq_blocks):
    inv_freq = 1.0 / (ROPE_THETA ** (jnp.arange(0, HEAD_DIM, 2, dtype=jnp.float32) / HEAD_DIM))
    freq = jnp.tile(inv_freq, LANES // HALF_HEAD)[None, :]
    row_ang = jnp.arange(rows, dtype=jnp.float32)[:, None] * freq
    start_ang = (jnp.arange(seq_blocks, dtype=jnp.float32) * rows)[:, None, None] * freq
    return jnp.cos(row_ang), jnp.sin(row_ang), jnp.cos(start_ang), jnp.sin(start_ang)


def _layer(x, g_mix, w_in, b_in, sinks, w_pool, b_pool, pool_scale, w_out, b_out,
           g_ffn, w_gate, w_up, w_down, g_final, *, final_norm):
    b, s, d = x.shape
    rows = ROWS
    assert s % rows == 0 and rows % BLOCK == 0 and d == D_MODEL and sum(IN_CHUNKS) == IN_WIDTH
    seq_blocks = s // rows
    n = b * seq_blocks
    row_cos, row_sin, start_cos, start_sin = _rope_tables(rows, seq_blocks)
    bf16 = jnp.bfloat16
    lead = WEIGHT_CHUNKS

    def prev(i, *_):
        return (jnp.clip(i - lead - 1, 0, n - 1), 0)

    def cur_start(i, *_):
        return (jnp.clip(i - lead, 0, n - 1) % seq_blocks, 0, 0)

    def nxt(i, *_):
        return (jnp.clip(i - lead + 1, 0, n - 1), 0)

    def weight_chunk(w):
        chunk = w.shape[0] // lead
        assert chunk * lead == w.shape[0] and chunk % BF16_SUBLANES == 0
        return pl.BlockSpec((chunk, w.shape[1]), lambda i, *_: (jnp.minimum(i, lead - 1), 0))

    x2d = x.reshape(b * s, d)
    weights = (w_in, w_out, w_gate, w_up, w_down)
    grid_spec = pltpu.PrefetchScalarGridSpec(
        num_scalar_prefetch=1,
        grid=(lead + n + 1,),
        in_specs=[
            pl.BlockSpec((rows, d), prev),
            pl.BlockSpec((rows, d), nxt),
            _resident((1, d)),
            weight_chunk(w_in),
            _resident((1, IN_WIDTH)),
            _resident((N_POOL_GROUPS, POOL_GROUP_WIDTH, POOL_GROUP_WIDTH)),
            _resident((1, POOL_WIDTH)),
            _resident((1, POOL_WIDTH)),
            weight_chunk(w_out),
            _resident((1, d)),
            _resident((rows, LANES)),
            _resident((rows, LANES)),
            pl.BlockSpec((None, 1, LANES), cur_start),
            pl.BlockSpec((None, 1, LANES), cur_start),
            _resident((1, d)),
            weight_chunk(w_gate),
            weight_chunk(w_up),
            weight_chunk(w_down),
            _resident((1, d)),
        ],
        out_specs=pl.BlockSpec((rows, d), prev),
        scratch_shapes=[pltpu.VMEM(w.shape, bf16) for w in weights] + [
            pltpu.VMEM((rows, d), bf16),
            pltpu.VMEM((rows, ATTN_WIDTH), bf16),
            pltpu.VMEM((N_KV_HEADS, rows + BLOCK, LANES), bf16),
            pltpu.VMEM((N_KV_HEADS, rows + BLOCK, LANES), bf16),
            pltpu.VMEM((POOL_HALO, POOL_WIDTH), jnp.float32),
            pltpu.VMEM((rows, POOL_WIDTH), bf16),
        ],
    )
    out = pl.pallas_call(
        functools.partial(_layer_kernel, w_chunks=lead, seq_blocks=seq_blocks, final_norm=final_norm),
        grid_spec=grid_spec,
        out_shape=jax.ShapeDtypeStruct((b * s, d), x.dtype),
        compiler_params=pltpu.CompilerParams(
            dimension_semantics=("arbitrary",),
            vmem_limit_bytes=VMEM_LIMIT_BYTES),
        name="layer",
    )(sinks.astype(jnp.float32), x2d, x2d, g_mix.reshape(1, d), w_in,
      b_in.reshape(1, IN_WIDTH), w_pool, b_pool.reshape(1, POOL_WIDTH),
      pool_scale.reshape(1, POOL_WIDTH), w_out, b_out.reshape(1, d),
      row_cos, row_sin, start_cos, start_sin,
      g_ffn.reshape(1, d), w_gate, w_up, w_down, g_final.reshape(1, d))
    return out.reshape(b, s, d)


def kernel(x, g_mix, w_in, b_in, sinks, w_pool, b_pool, pool_scale, w_out, b_out, g_ffn, w_gate, w_up, w_down, g_final):
    depth = g_mix.shape[0]
    for layer in range(depth):
        x = _layer(x, g_mix[layer], w_in[layer], b_in[layer], sinks[layer], w_pool[layer],
                   b_pool[layer], pool_scale[layer], w_out[layer], b_out[layer],
                   g_ffn[layer], w_gate[layer], w_up[layer], w_down[layer], g_final,
                   final_norm=layer == depth - 1)
    return x
```

```python
import functools

import jax
import jax.numpy as jnp
from jax import lax
from jax.experimental import pallas as pl
from jax.experimental.pallas import tpu as pltpu

D_MODEL = 1024
HEAD_DIM = 64
HALF_HEAD = HEAD_DIM // 2
N_Q_HEADS = 8
N_KV_HEADS = 2
GQA_GROUP = N_Q_HEADS // N_KV_HEADS
ATTN_WIDTH = N_Q_HEADS * HEAD_DIM
KV_WIDTH = N_KV_HEADS * HEAD_DIM
POOL_SIZES = (2, 4, 8, 16)
N_POOL_GROUPS = len(POOL_SIZES)
POOL_GROUP_WIDTH = 128
POOL_WIDTH = N_POOL_GROUPS * POOL_GROUP_WIDTH
IN_WIDTH = ATTN_WIDTH + 2 * KV_WIDTH + POOL_WIDTH
WINDOW = 128
BLOCK = 128
ROPE_THETA = 10000.0
RMS_EPS = 1e-5
LOG2_E = 1.4426950408889634
POOL_HALO = 16

LANES = 128
MXU_WIDTH = 256
IN_CHUNKS = (256, 256, 256, 256, 256)
VMEM_LIMIT_BYTES = 60 * 1024 * 1024

BF16_SUBLANES = 16
ROWS = 512
WEIGHT_CHUNKS = 8


def _rmsnorm(x, g):
    ms = jnp.mean(x * x, axis=-1, keepdims=True)
    return x * lax.rsqrt(ms + RMS_EPS) * g


def _residual_projection(residual, lhs, w_ref, bias_ref=None):
    chunks, sumsq = [], None
    for c in range(w_ref.shape[1] // MXU_WIDTH):
        cols = slice(c * MXU_WIDTH, (c + 1) * MXU_WIDTH)
        y = residual(cols) + jnp.dot(lhs, w_ref[:, cols], preferred_element_type=jnp.float32)
        if bias_ref is not None:
            y = y + bias_ref[:, cols]
        part = jnp.sum(y * y, axis=-1, keepdims=True)
        sumsq = part if sumsq is None else sumsq + part
        chunks.append(y)
    return chunks, sumsq


def _rmsnorm_chunks(chunks, sumsq, g_ref):
    width = sum(y.shape[1] for y in chunks)
    r = lax.rsqrt(sumsq * (1.0 / width) + RMS_EPS)
    return [y * r * g_ref[:, c * MXU_WIDTH:(c + 1) * MXU_WIDTH] for c, y in enumerate(chunks)]


def _ffn_block(x_chunks, x_sumsq, g_ref, wg_ref, wu_ref, wd_ref, gf_ref, final_norm):
    width = sum(y.shape[1] for y in x_chunks)
    xg = jnp.concatenate([y * g_ref[:, c * MXU_WIDTH:(c + 1) * MXU_WIDTH] for c, y in enumerate(x_chunks)],
                         axis=1).astype(jnp.bfloat16)
    r = lax.rsqrt(x_sumsq * (1.0 / width) + RMS_EPS)
    gate = r * jnp.dot(xg, wg_ref[...], preferred_element_type=jnp.float32)
    up = r * jnp.dot(xg, wu_ref[...], preferred_element_type=jnp.float32)
    act = (gate * jax.nn.sigmoid(gate) * up).astype(jnp.bfloat16)
    y_chunks, y_sumsq = _residual_projection(lambda cols: x_chunks[cols.start // MXU_WIDTH], act, wd_ref)
    return _rmsnorm_chunks(y_chunks, y_sumsq, gf_ref) if final_norm else y_chunks


def _attention_block(j, seq_block, sinks_ref, q_ref, kext_ref, vext_ref):
    rows = q_ref.shape[0]
    qi = lax.broadcasted_iota(jnp.int32, (BLOCK, 2 * BLOCK), 0) + BLOCK
    sj = lax.broadcasted_iota(jnp.int32, (BLOCK, 2 * BLOCK), 1)
    delta = qi - sj
    key_pos = seq_block * rows + j * BLOCK + sj - BLOCK
    visible = (delta >= 0) & (delta < WINDOW) & (key_pos >= 0)
    bias = jnp.where(visible, 0.0, -jnp.inf).astype(jnp.float32)
    low_b = lax.broadcasted_iota(jnp.int32, (BLOCK, LANES), 1) < HEAD_DIM
    pairs_per_group = GQA_GROUP // 2

    pair_outs = []
    for g in range(N_KV_HEADS):
        kband = kext_ref[g, j * BLOCK:(j + 2) * BLOCK, :]
        vband = vext_ref[g, j * BLOCK:(j + 2) * BLOCK, :]
        qm = []
        for p in range(pairs_per_group):
            c = (g * pairs_per_group + p) * LANES
            qp = q_ref[j * BLOCK:(j + 1) * BLOCK, c:c + LANES]
            qm.append(jnp.where(low_b, qp, jnp.zeros_like(qp)))
            qm.append(jnp.where(low_b, jnp.zeros_like(qp), qp))
        qm = jnp.concatenate(qm, axis=0)
        s = lax.dot_general(qm, kband, (((1,), (1,)), ((), ())),
                            preferred_element_type=jnp.float32)
        ps, inv_ls = [], []
        for hh in range(GQA_GROUP):
            sink = sinks_ref[g * GQA_GROUP + hh] * LOG2_E
            s_h = s[hh * BLOCK:(hh + 1) * BLOCK] + bias
            m = jnp.maximum(jnp.max(s_h, axis=-1, keepdims=True), sink)
            p_h = jnp.exp2(s_h - m)
            l = jnp.sum(p_h, axis=-1, keepdims=True) + jnp.exp2(sink - m)
            ps.append(p_h.astype(jnp.bfloat16))
            inv_ls.append(1.0 / l)
        pm = jnp.concatenate(ps, axis=0)
        o = jnp.dot(pm, vband, preferred_element_type=jnp.float32)
        for p in range(pairs_per_group):
            o_a = o[(2 * p) * BLOCK:(2 * p + 1) * BLOCK] * inv_ls[2 * p]
            o_b = o[(2 * p + 1) * BLOCK:(2 * p + 2) * BLOCK] * inv_ls[2 * p + 1]
            pair_outs.append(jnp.where(low_b, o_a, o_b))
    return jnp.concatenate(pair_outs, axis=1)


def _store_projection(z, seq_block, wpool_ref, bpool_ref, pscale_ref, rcos_ref, rsin_ref, bcos_ref, bsin_ref,
                      q_ref, kext_ref, vext_ref, utail_ref, pool_ref):
    rows = z[0].shape[0]
    lane = lax.broadcasted_iota(jnp.int32, (rows, LANES), 1)
    low_half = lane < HEAD_DIM
    first_half = (lane % HEAD_DIM) < HALF_HEAD
    rcos, rsin, bcos, bsin = rcos_ref[...], rsin_ref[...], bcos_ref[...], bsin_ref[...]
    cos = rcos * bcos - rsin * bsin
    sin = jnp.where(first_half, -1.0, 1.0) * (rsin * bcos + rcos * bsin)

    def rope(t):
        partner = jnp.where(first_half, pltpu.roll(t, LANES - HALF_HEAD, 1), pltpu.roll(t, HALF_HEAD, 1))
        return t * cos + partner * sin

    q_pieces = ATTN_WIDTH // LANES
    scale = HEAD_DIM ** -0.5 * LOG2_E
    for p in range(q_pieces):
        q_ref[:, p * LANES:(p + 1) * LANES] = (rope(z[p]) * scale).astype(q_ref.dtype)

    keep = seq_block > 0
    for ref in (kext_ref, vext_ref):
        tail = ref[:, rows:rows + BLOCK, :]
        ref[:, 0:BLOCK, :] = jnp.where(keep, tail, jnp.zeros_like(tail))

    k = rope(z[q_pieces])
    v = z[q_pieces + 1]
    k_sw = pltpu.roll(k, HEAD_DIM, 1)
    v_sw = pltpu.roll(v, HEAD_DIM, 1)
    kext_ref[0, BLOCK:, :] = jnp.where(low_half, k, k_sw).astype(kext_ref.dtype)
    kext_ref[1, BLOCK:, :] = jnp.where(low_half, k_sw, k).astype(kext_ref.dtype)
    vext_ref[0, BLOCK:, :] = jnp.where(low_half, v, v_sw).astype(vext_ref.dtype)
    vext_ref[1, BLOCK:, :] = jnp.where(low_half, v_sw, v).astype(vext_ref.dtype)

    halo = utail_ref[...]
    halo = jnp.where(keep, halo, jnp.zeros_like(halo))
    pos = seq_block * rows + lax.broadcasted_iota(jnp.int32, (rows, LANES), 0)
    for gi, size in enumerate(POOL_SIZES):
        cols = slice(gi * LANES, (gi + 1) * LANES)
        u_g = z[q_pieces + 2 + gi]
        utail_ref[:, cols] = u_g[rows - POOL_HALO:]
        acc = jnp.concatenate([halo[:, cols], u_g], axis=0)
        span = 1
        while span < size:
            acc = acc + pltpu.roll(acc, span, 0)
            span *= 2
        count = jnp.minimum(pos + 1, size).astype(jnp.float32)
        mixed = acc[POOL_HALO:] / count - u_g
        y = jnp.dot(mixed.astype(jnp.bfloat16), wpool_ref[gi].astype(jnp.bfloat16),
                    preferred_element_type=jnp.float32)
        pool_ref[:, cols] = ((y + bpool_ref[:, cols]) * pscale_ref[:, cols]).astype(pool_ref.dtype)


def _layer_step(i, sinks_ref, xprev_ref, xnext_ref, gmix_ref, bin_ref, wpool_ref, bpool_ref, pscale_ref,
                bout_ref, rcos_ref, rsin_ref, bcos_ref, bsin_ref, gffn_ref, gf_ref, o_ref, win_ref, wout_ref, wg_ref, wu_ref, wd_ref,
                h_ref, q_ref, kext_ref, vext_ref, utail_ref, pool_ref, *, seq_blocks, final_norm):
    rows = o_ref.shape[0]
    chunk_starts = [sum(IN_CHUNKS[:t]) for t in range(len(IN_CHUNKS))]
    n_chunks = len(IN_CHUNKS)
    n_attn = rows // BLOCK
    project_refs = (wpool_ref, bpool_ref, pscale_ref, rcos_ref, rsin_ref, bcos_ref, bsin_ref,
                    q_ref, kext_ref, vext_ref, utail_ref, pool_ref)

    def in_projection_chunk(h, t):
        cols = slice(chunk_starts[t], chunk_starts[t] + IN_CHUNKS[t])
        zc = jnp.dot(h, win_ref[:, cols], preferred_element_type=jnp.float32) + bin_ref[:, cols]
        return [zc[:, c * LANES:(c + 1) * LANES] for c in range(IN_CHUNKS[t] // LANES)]

    @pl.when(i == 0)
    def _():
        kext_ref[...] = jnp.zeros(kext_ref.shape, kext_ref.dtype)
        vext_ref[...] = jnp.zeros(vext_ref.shape, vext_ref.dtype)
        utail_ref[...] = jnp.zeros(utail_ref.shape, utail_ref.dtype)
        h = _rmsnorm(xprev_ref[...], gmix_ref[...]).astype(h_ref.dtype)
        z = sum((in_projection_chunk(h, t) for t in range(n_chunks)), [])
        _store_projection(z, 0, *project_refs)
        h_ref[...] = _rmsnorm(xnext_ref[...], gmix_ref[...]).astype(h_ref.dtype)

    @pl.when(i > 0)
    def _():
        prev_seq_block = (i + seq_blocks - 1) % seq_blocks
        h = h_ref[...]
        z, attn = [], []
        for t in range(max(n_chunks, n_attn)):
            if t < n_attn:
                attn.append(_attention_block(t, prev_seq_block, sinks_ref, q_ref, kext_ref, vext_ref))
            if t < n_chunks:
                z += in_projection_chunk(h, t)

        cat = jnp.concatenate([jnp.concatenate(attn, axis=0).astype(jnp.bfloat16), pool_ref[...]], axis=1)
        x1_chunks, x1_sumsq = _residual_projection(lambda cols: xprev_ref[:, cols], cat, wout_ref, bout_ref)

        _store_projection(z, i % seq_blocks, *project_refs)

        h_ref[...] = _rmsnorm(xnext_ref[...], gmix_ref[...]).astype(h_ref.dtype)

        out_chunks = _ffn_block(x1_chunks, x1_sumsq, gffn_ref, wg_ref, wu_ref, wd_ref, gf_ref, final_norm)
        for c, y in enumerate(out_chunks):
            o_ref[:, c * MXU_WIDTH:(c + 1) * MXU_WIDTH] = y


def _layer_kernel(sinks_ref, xprev_ref, xnext_ref, gmix_ref, win32_ref, bin_ref, wpool_ref, bpool_ref,
                  pscale_ref, wout32_ref, bout_ref, rcos_ref, rsin_ref, bcos_ref, bsin_ref, gffn_ref, wg32_ref, wu32_ref, wd32_ref,
                  gf_ref, o_ref, win_ref, wout_ref, wg_ref, wu_ref, wd_ref, *scratch, w_chunks, **static):
    i = pl.program_id(0)

    @pl.when(i < w_chunks)
    def _():
        for src, dst in ((win32_ref, win_ref), (wout32_ref, wout_ref), (wg32_ref, wg_ref),
                         (wu32_ref, wu_ref), (wd32_ref, wd_ref)):
            chunk = src.shape[0]
            dst[pl.ds(pl.multiple_of(i * chunk, chunk), chunk), :] = src[...].astype(dst.dtype)

    @pl.when(i >= w_chunks)
    def _():
        _layer_step(i - w_chunks, sinks_ref, xprev_ref, xnext_ref, gmix_ref, bin_ref, wpool_ref, bpool_ref,
                    pscale_ref, bout_ref, rcos_ref, rsin_ref, bcos_ref, bsin_ref, gffn_ref, gf_ref, o_ref, win_ref, wout_ref,
                    wg_ref, wu_ref, wd_ref, *scratch, **static)


def _resident(shape):
    zeros = (0,) * len(shape)
    return pl.BlockSpec(shape, lambda *_: zeros, pipeline_mode=pl.Buffered(1))


def _rope_tables(rows, seq_blocks):
    inv_freq = 1.0 / (ROPE_THETA ** (jnp.arange(0, HEAD_DIM, 2, dtype=jnp.float32) / HEAD_DIM))
    freq = jnp.tile(inv_freq, LANES // HALF_HEAD)[None, :]
    row_ang = jnp.arange(rows, dtype=jnp.float32)[:, None] * freq
    start_ang = (jnp.arange(seq_blocks, dtype=jnp.float32) * rows)[:, None, None] * freq
    return jnp.cos(row_ang), jnp.sin(row_ang), jnp.cos(start_ang), jnp.sin(start_ang)


def _layer(x, g_mix, w_in, b_in, sinks, w_pool, b_pool, pool_scale, w_out, b_out,
           g_ffn, w_gate, w_up, w_down, g_final, *, final_norm):
    b, s, d = x.shape
    rows = ROWS
    assert s % rows == 0 and rows % BLOCK == 0 and d == D_MODEL and sum(IN_CHUNKS) == IN_WIDTH
    seq_blocks = s // rows
    n = b * seq_blocks
    row_cos, row_sin, start_cos, start_sin = _rope_tables(rows, seq_blocks)
    bf16 = jnp.bfloat16
    lead = WEIGHT_CHUNKS

    def prev(i, *_):
        return (jnp.clip(i - lead - 1, 0, n - 1), 0)

    def cur_start(i, *_):
        return (jnp.clip(i - lead, 0, n - 1) % seq_blocks, 0, 0)

    def nxt(i, *_):
        return (jnp.clip(i - lead + 1, 0, n - 1), 0)

    def weight_chunk(w):
        chunk = w.shape[0] // lead
        assert chunk * lead == w.shape[0] and chunk % BF16_SUBLANES == 0
        return pl.BlockSpec((chunk, w.shape[1]), lambda i, *_: (jnp.minimum(i, lead - 1), 0))

    x2d = x.reshape(b * s, d)
    weights = (w_in, w_out, w_gate, w_up, w_down)
    grid_spec = pltpu.PrefetchScalarGridSpec(
        num_scalar_prefetch=1,
        grid=(lead + n + 1,),
        in_specs=[
            pl.BlockSpec((rows, d), prev),
            pl.BlockSpec((rows, d), nxt),
            _resident((1, d)),
            weight_chunk(w_in),
            _resident((1, IN_WIDTH)),
            _resident((N_POOL_GROUPS, POOL_GROUP_WIDTH, POOL_GROUP_WIDTH)),
            _resident((1, POOL_WIDTH)),
            _resident((1, POOL_WIDTH)),
            weight_chunk(w_out),
            _resident((1, d)),
            _resident((rows, LANES)),
            _resident((rows, LANES)),
            pl.BlockSpec((None, 1, LANES), cur_start),
            pl.BlockSpec((None, 1, LANES), cur_start),
            _resident((1, d)),
            weight_chunk(w_gate),
            weight_chunk(w_up),
            weight_chunk(w_down),
            _resident((1, d)),
        ],
        out_specs=pl.BlockSpec((rows, d), prev),
        scratch_shapes=[pltpu.VMEM(w.shape, bf16) for w in weights] + [
            pltpu.VMEM((rows, d), bf16),
            pltpu.VMEM((rows, ATTN_WIDTH), bf16),
            pltpu.VMEM((N_KV_HEADS, rows + BLOCK, LANES), bf16),
            pltpu.VMEM((N_KV_HEADS, rows + BLOCK, LANES), bf16),
            pltpu.VMEM((POOL_HALO, POOL_WIDTH), jnp.float32),
            pltpu.VMEM((rows, POOL_WIDTH), bf16),
        ],
    )
    out = pl.pallas_call(
        functools.partial(_layer_kernel, w_chunks=lead, seq_blocks=seq_blocks, final_norm=final_norm),
        grid_spec=grid_spec,
        out_shape=jax.ShapeDtypeStruct((b * s, d), x.dtype),
        compiler_params=pltpu.CompilerParams(
            dimension_semantics=("arbitrary",),
            vmem_limit_bytes=VMEM_LIMIT_BYTES),
        name="layer",
    )(sinks.astype(jnp.float32), x2d, x2d, g_mix.reshape(1, d), w_in,
      b_in.reshape(1, IN_WIDTH), w_pool, b_pool.reshape(1, POOL_WIDTH),
      pool_scale.reshape(1, POOL_WIDTH), w_out, b_out.reshape(1, d),
      row_cos, row_sin, start_cos, start_sin,
      g_ffn.reshape(1, d), w_gate, w_up, w_down, g_final.reshape(1, d))
    return out.reshape(b, s, d)


def kernel(x, g_mix, w_in, b_in, sinks, w_pool, b_pool, pool_scale, w_out, b_out, g_ffn, w_gate, w_up, w_down, g_final):
    depth = g_mix.shape[0]
    for layer in range(depth):
        x = _layer(x, g_mix[layer], w_in[layer], b_in[layer], sinks[layer], w_pool[layer],
                   b_pool[layer], pool_scale[layer], w_out[layer], b_out[layer],
                   g_ffn[layer], w_gate[layer], w_up[layer], w_down[layer], g_final,
                   final_norm=layer == depth - 1)
    return x
```

```python
import functools

import jax
import jax.numpy as jnp
from jax import lax
from jax.experimental import pallas as pl
from jax.experimental.pallas import tpu as pltpu

D_MODEL = 1024
HEAD_DIM = 64
HALF_HEAD = HEAD_DIM // 2
N_Q_HEADS = 8
N_KV_HEADS = 2
GQA_GROUP = N_Q_HEADS // N_KV_HEADS
ATTN_WIDTH = N_Q_HEADS * HEAD_DIM
KV_WIDTH = N_KV_HEADS * HEAD_DIM
POOL_SIZES = (2, 4, 8, 16)
N_POOL_GROUPS = len(POOL_SIZES)
POOL_GROUP_WIDTH = 128
POOL_WIDTH = N_POOL_GROUPS * POOL_GROUP_WIDTH
IN_WIDTH = ATTN_WIDTH + 2 * KV_WIDTH + POOL_WIDTH
WINDOW = 128
BLOCK = 128
ROPE_THETA = 10000.0
RMS_EPS = 1e-5
LOG2_E = 1.4426950408889634
POOL_HALO = 16

LANES = 128
MXU_WIDTH = 256
IN_CHUNKS = (256, 256, 256, 256, 256)
VMEM_LIMIT_BYTES = 60 * 1024 * 1024

BF16_SUBLANES = 16
ROWS = 512
WEIGHT_CHUNKS = 8


def _rmsnorm(x, g):
    ms = jnp.mean(x * x, axis=-1, keepdims=True)
    return x * lax.rsqrt(ms + RMS_EPS) * g


def _residual_projection(residual, lhs, w_ref, bias_ref=None):
    chunks, sumsq = [], None
    for c in range(w_ref.shape[1] // MXU_WIDTH):
        cols = slice(c * MXU_WIDTH, (c + 1) * MXU_WIDTH)
        y = residual(cols) + jnp.dot(lhs, w_ref[:, cols], preferred_element_type=jnp.float32)
        if bias_ref is not None:
            y = y + bias_ref[:, cols]
        part = jnp.sum(y * y, axis=-1, keepdims=True)
        sumsq = part if sumsq is None else sumsq + part
        chunks.append(y)
    return chunks, sumsq


def _rmsnorm_chunks(chunks, sumsq, g_ref):
    width = sum(y.shape[1] for y in chunks)
    r = lax.rsqrt(sumsq * (1.0 / width) + RMS_EPS)
    return [y * r * g_ref[:, c * MXU_WIDTH:(c + 1) * MXU_WIDTH] for c, y in enumerate(chunks)]


def _ffn_block(x_chunks, x_sumsq, g_ref, wg_ref, wu_ref, wd_ref, gf_ref, final_norm):
    width = sum(y.shape[1] for y in x_chunks)
    xg = jnp.concatenate([y * g_ref[:, c * MXU_WIDTH:(c + 1) * MXU_WIDTH] for c, y in enumerate(x_chunks)],
                         axis=1).astype(jnp.bfloat16)
    r = lax.rsqrt(x_sumsq * (1.0 / width) + RMS_EPS)
    gate = r * jnp.dot(xg, wg_ref[...], preferred_element_type=jnp.float32)
    up = r * jnp.dot(xg, wu_ref[...], preferred_element_type=jnp.float32)
    act = (gate * jax.nn.sigmoid(gate) * up).astype(jnp.bfloat16)
    y_chunks, y_sumsq = _residual_projection(lambda cols: x_chunks[cols.start // MXU_WIDTH], act, wd_ref)
    return _rmsnorm_chunks(y_chunks, y_sumsq, gf_ref) if final_norm else y_chunks


def _attention_block(j, seq_block, sinks_ref, q_ref, kext_ref, vext_ref):
    rows = q_ref.shape[0]
    qi = lax.broadcasted_iota(jnp.int32, (BLOCK, 2 * BLOCK), 0) + BLOCK
    sj = lax.broadcasted_iota(jnp.int32, (BLOCK, 2 * BLOCK), 1)
    delta = qi - sj
    key_pos = seq_block * rows + j * BLOCK + sj - BLOCK
    visible = (delta >= 0) & (delta < WINDOW) & (key_pos >= 0)
    bias = jnp.where(visible, 0.0, -jnp.inf).astype(jnp.float32)
    low_b = lax.broadcasted_iota(jnp.int32, (BLOCK, LANES), 1) < HEAD_DIM
    pairs_per_group = GQA_GROUP // 2

    pair_outs = []
    for g in range(N_KV_HEADS):
        kband = kext_ref[g, j * BLOCK:(j + 2) * BLOCK, :]
        vband = vext_ref[g, j * BLOCK:(j + 2) * BLOCK, :]
        qm = []
        for p in range(pairs_per_group):
            c = (g * pairs_per_group + p) * LANES
            qp = q_ref[j * BLOCK:(j + 1) * BLOCK, c:c + LANES]
            qm.append(jnp.where(low_b, qp, jnp.zeros_like(qp)))
            qm.append(jnp.where(low_b, jnp.zeros_like(qp), qp))
        qm = jnp.concatenate(qm, axis=0)
        s = lax.dot_general(qm, kband, (((1,), (1,)), ((), ())),
                            preferred_element_type=jnp.float32)
        ps, inv_ls = [], []
        for hh in range(GQA_GROUP):
            sink = sinks_ref[g * GQA_GROUP + hh] * LOG2_E
            s_h = s[hh * BLOCK:(hh + 1) * BLOCK] + bias
            m = jnp.maximum(jnp.max(s_h, axis=-1, keepdims=True), sink)
            p_h = jnp.exp2(s_h - m)
            l = jnp.sum(p_h, axis=-1, keepdims=True) + jnp.exp2(sink - m)
            ps.append(p_h.astype(jnp.bfloat16))
            inv_ls.append(1.0 / l)
        pm = jnp.concatenate(ps, axis=0)
        o = jnp.dot(pm, vband, preferred_element_type=jnp.float32)
        for p in range(pairs_per_group):
            o_a = o[(2 * p) * BLOCK:(2 * p + 1) * BLOCK] * inv_ls[2 * p]
            o_b = o[(2 * p + 1) * BLOCK:(2 * p + 2) * BLOCK] * inv_ls[2 * p + 1]
            pair_outs.append(jnp.where(low_b, o_a, o_b))
    return jnp.concatenate(pair_outs, axis=1)


def _store_projection(z, seq_block, wpool_ref, bpool_ref, pscale_ref, rcos_ref, rsin_ref, bcos_ref, bsin_ref,
                      q_ref, kext_ref, vext_ref, utail_ref, pool_ref):
    rows = z[0].shape[0]
    lane = lax.broadcasted_iota(jnp.int32, (rows, LANES), 1)
    low_half = lane < HEAD_DIM
    first_half = (lane % HEAD_DIM) < HALF_HEAD
    rcos, rsin, bcos, bsin = rcos_ref[...], rsin_ref[...], bcos_ref[...], bsin_ref[...]
    cos = rcos * bcos - rsin * bsin
    sin = jnp.where(first_half, -1.0, 1.0) * (rsin * bcos + rcos * bsin)

    def rope(t):
        partner = jnp.where(first_half, pltpu.roll(t, LANES - HALF_HEAD, 1), pltpu.roll(t, HALF_HEAD, 1))
        return t * cos + partner * sin

    q_pieces = ATTN_WIDTH // LANES
    scale = HEAD_DIM ** -0.5 * LOG2_E
    for p in range(q_pieces):
        q_ref[:, p * LANES:(p + 1) * LANES] = (rope(z[p]) * scale).astype(q_ref.dtype)

    keep = seq_block > 0
    for ref in (kext_ref, vext_ref):
        tail = ref[:, rows:rows + BLOCK, :]
        ref[:, 0:BLOCK, :] = jnp.where(keep, tail, jnp.zeros_like(tail))

    k = rope(z[q_pieces])
    v = z[q_pieces + 1]
    k_sw = pltpu.roll(k, HEAD_DIM, 1)
    v_sw = pltpu.roll(v, HEAD_DIM, 1)
    kext_ref[0, BLOCK:, :] = jnp.where(low_half, k, k_sw).astype(kext_ref.dtype)
    kext_ref[1, BLOCK:, :] = jnp.where(low_half, k_sw, k).astype(kext_ref.dtype)
    vext_ref[0, BLOCK:, :] = jnp.where(low_half, v, v_sw).astype(vext_ref.dtype)
    vext_ref[1, BLOCK:, :] = jnp.where(low_half, v_sw, v).astype(vext_ref.dtype)

    halo = utail_ref[...]
    halo = jnp.where(keep, halo, jnp.zeros_like(halo))
    pos = seq_block * rows + lax.broadcasted_iota(jnp.int32, (rows, LANES), 0)
    for gi, size in enumerate(POOL_SIZES):
        cols = slice(gi * LANES, (gi + 1) * LANES)
        u_g = z[q_pieces + 2 + gi]
        utail_ref[:, cols] = u_g[rows - POOL_HALO:]
        acc = jnp.concatenate([halo[:, cols], u_g], axis=0)
        span = 1
        while span < size:
            acc = acc + pltpu.roll(acc, span, 0)
            span *= 2
        count = jnp.minimum(pos + 1, size).astype(jnp.float32)
        mixed = acc[POOL_HALO:] / count - u_g
        y = jnp.dot(mixed.astype(jnp.bfloat16), wpool_ref[gi].astype(jnp.bfloat16),
                    preferred_element_type=jnp.float32)
        pool_ref[:, cols] = ((y + bpool_ref[:, cols]) * pscale_ref[:, cols]).astype(pool_ref.dtype)


def _layer_step(i, sinks_ref, xprev_ref, xnext_ref, gmix_ref, bin_ref, wpool_ref, bpool_ref, pscale_ref,
                bout_ref, rcos_ref, rsin_ref, bcos_ref, bsin_ref, gffn_ref, gf_ref, o_ref, win_ref, wout_ref, wg_ref, wu_ref, wd_ref,
                h_ref, q_ref, kext_ref, vext_ref, utail_ref, pool_ref, *, seq_blocks, final_norm):
    rows = o_ref.shape[0]
    chunk_starts = [sum(IN_CHUNKS[:t]) for t in range(len(IN_CHUNKS))]
    n_chunks = len(IN_CHUNKS)
    n_attn = rows // BLOCK
    project_refs = (wpool_ref, bpool_ref, pscale_ref, rcos_ref, rsin_ref, bcos_ref, bsin_ref,
                    q_ref, kext_ref, vext_ref, utail_ref, pool_ref)

    def in_projection_chunk(h, t):
        cols = slice(chunk_starts[t], chunk_starts[t] + IN_CHUNKS[t])
        half = h.shape[0] // 2
        zc = jnp.concatenate([jnp.dot(h[:half], win_ref[:, cols], preferred_element_type=jnp.float32),
                              jnp.dot(h[half:], win_ref[:, cols], preferred_element_type=jnp.float32)],
                             axis=0) + bin_ref[:, cols]
        return [zc[:, c * LANES:(c + 1) * LANES] for c in range(IN_CHUNKS[t] // LANES)]

    @pl.when(i == 0)
    def _():
        kext_ref[...] = jnp.zeros(kext_ref.shape, kext_ref.dtype)
        vext_ref[...] = jnp.zeros(vext_ref.shape, vext_ref.dtype)
        utail_ref[...] = jnp.zeros(utail_ref.shape, utail_ref.dtype)
        h = _rmsnorm(xprev_ref[...], gmix_ref[...]).astype(h_ref.dtype)
        z = sum((in_projection_chunk(h, t) for t in range(n_chunks)), [])
        _store_projection(z, 0, *project_refs)
        h_ref[...] = _rmsnorm(xnext_ref[...], gmix_ref[...]).astype(h_ref.dtype)

    @pl.when(i > 0)
    def _():
        prev_seq_block = (i + seq_blocks - 1) % seq_blocks
        h = h_ref[...]
        z, attn = [], []
        for t in range(max(n_chunks, n_attn)):
            if t < n_attn:
                attn.append(_attention_block(t, prev_seq_block, sinks_ref, q_ref, kext_ref, vext_ref))
            if t < n_chunks:
                z += in_projection_chunk(h, t)

        cat = jnp.concatenate([jnp.concatenate(attn, axis=0).astype(jnp.bfloat16), pool_ref[...]], axis=1)
        x1_chunks, x1_sumsq = _residual_projection(lambda cols: xprev_ref[:, cols], cat, wout_ref, bout_ref)

        _store_projection(z, i % seq_blocks, *project_refs)

        h_ref[...] = _rmsnorm(xnext_ref[...], gmix_ref[...]).astype(h_ref.dtype)

        out_chunks = _ffn_block(x1_chunks, x1_sumsq, gffn_ref, wg_ref, wu_ref, wd_ref, gf_ref, final_norm)
        for c, y in enumerate(out_chunks):
            o_ref[:, c * MXU_WIDTH:(c + 1) * MXU_WIDTH] = y


def _layer_kernel(sinks_ref, xprev_ref, xnext_ref, gmix_ref, win32_ref, bin_ref, wpool_ref, bpool_ref,
                  pscale_ref, wout32_ref, bout_ref, rcos_ref, rsin_ref, bcos_ref, bsin_ref, gffn_ref, wg32_ref, wu32_ref, wd32_ref,
                  gf_ref, o_ref, win_ref, wout_ref, wg_ref, wu_ref, wd_ref, *scratch, w_chunks, **static):
    i = pl.program_id(0)

    @pl.when(i < w_chunks)
    def _():
        for src, dst in ((win32_ref, win_ref), (wout32_ref, wout_ref), (wg32_ref, wg_ref),
                         (wu32_ref, wu_ref), (wd32_ref, wd_ref)):
            chunk = src.shape[0]
            dst[pl.ds(pl.multiple_of(i * chunk, chunk), chunk), :] = src[...].astype(dst.dtype)

    @pl.when(i >= w_chunks)
    def _():
        _layer_step(i - w_chunks, sinks_ref, xprev_ref, xnext_ref, gmix_ref, bin_ref, wpool_ref, bpool_ref,
                    pscale_ref, bout_ref, rcos_ref, rsin_ref, bcos_ref, bsin_ref, gffn_ref, gf_ref, o_ref, win_ref, wout_ref,
                    wg_ref, wu_ref, wd_ref, *scratch, **static)


def _resident(shape):
    zeros = (0,) * len(shape)
    return pl.BlockSpec(shape, lambda *_: zeros, pipeline_mode=pl.Buffered(1))


def _rope_tables(rows, seq_blocks):
    inv_freq = 1.0 / (ROPE_THETA ** (jnp.arange(0, HEAD_DIM, 2, dtype=jnp.float32) / HEAD_DIM))
    freq = jnp.tile(inv_freq, LANES // HALF_HEAD)[None, :]
    row_ang = jnp.arange(rows, dtype=jnp.float32)[:, None] * freq
    start_ang = (jnp.arange(seq_blocks, dtype=jnp.float32) * rows)[:, None, None] * freq
    return jnp.cos(row_ang), jnp.sin(row_ang), jnp.cos(start_ang), jnp.sin(start_ang)


def _layer(x, g_mix, w_in, b_in, sinks, w_pool, b_pool, pool_scale, w_out, b_out,
           g_ffn, w_gate, w_up, w_down, g_final, *, final_norm):
    b, s, d = x.shape
    rows = ROWS
    assert s % rows == 0 and rows % BLOCK == 0 and d == D_MODEL and sum(IN_CHUNKS) == IN_WIDTH
    seq_blocks = s // rows
    n = b * seq_blocks
    row_cos, row_sin, start_cos, start_sin = _rope_tables(rows, seq_blocks)
    bf16 = jnp.bfloat16
    lead = WEIGHT_CHUNKS

    def prev(i, *_):
        return (jnp.clip(i - lead - 1, 0, n - 1), 0)

    def cur_start(i, *_):
        return (jnp.clip(i - lead, 0, n - 1) % seq_blocks, 0, 0)

    def nxt(i, *_):
        return (jnp.clip(i - lead + 1, 0, n - 1), 0)

    def weight_chunk(w):
        chunk = w.shape[0] // lead
        assert chunk * lead == w.shape[0] and chunk % BF16_SUBLANES == 0
        return pl.BlockSpec((chunk, w.shape[1]), lambda i, *_: (jnp.minimum(i, lead - 1), 0))

    x2d = x.reshape(b * s, d)
    weights = (w_in, w_out, w_gate, w_up, w_down)
    grid_spec = pltpu.PrefetchScalarGridSpec(
        num_scalar_prefetch=1,
        grid=(lead + n + 1,),
        in_specs=[
            pl.BlockSpec((rows, d), prev),
            pl.BlockSpec((rows, d), nxt),
            _resident((1, d)),
            weight_chunk(w_in),
            _resident((1, IN_WIDTH)),
            _resident((N_POOL_GROUPS, POOL_GROUP_WIDTH, POOL_GROUP_WIDTH)),
            _resident((1, POOL_WIDTH)),
            _resident((1, POOL_WIDTH)),
            weight_chunk(w_out),
            _resident((1, d)),
            _resident((rows, LANES)),
            _resident((rows, LANES)),
            pl.BlockSpec((None, 1, LANES), cur_start),
            pl.BlockSpec((None, 1, LANES), cur_start),
            _resident((1, d)),
            weight_chunk(w_gate),
            weight_chunk(w_up),
            weight_chunk(w_down),
            _resident((1, d)),
        ],
        out_specs=pl.BlockSpec((rows, d), prev),
        scratch_shapes=[pltpu.VMEM(w.shape, bf16) for w in weights] + [
            pltpu.VMEM((rows, d), bf16),
            pltpu.VMEM((rows, ATTN_WIDTH), bf16),
            pltpu.VMEM((N_KV_HEADS, rows + BLOCK, LANES), bf16),
            pltpu.VMEM((N_KV_HEADS, rows + BLOCK, LANES), bf16),
            pltpu.VMEM((POOL_HALO, POOL_WIDTH), jnp.float32),
            pltpu.VMEM((rows, POOL_WIDTH), bf16),
        ],
    )
    out = pl.pallas_call(
        functools.partial(_layer_kernel, w_chunks=lead, seq_blocks=seq_blocks, final_norm=final_norm),
        grid_spec=grid_spec,
        out_shape=jax.ShapeDtypeStruct((b * s, d), x.dtype),
        compiler_params=pltpu.CompilerParams(
            dimension_semantics=("arbitrary",),
            vmem_limit_bytes=VMEM_LIMIT_BYTES),
        name="layer",
    )(sinks.astype(jnp.float32), x2d, x2d, g_mix.reshape(1, d), w_in,
      b_in.reshape(1, IN_WIDTH), w_pool, b_pool.reshape(1, POOL_WIDTH),
      pool_scale.reshape(1, POOL_WIDTH), w_out, b_out.reshape(1, d),
      row_cos, row_sin, start_cos, start_sin,
      g_ffn.reshape(1, d), w_gate, w_up, w_down, g_final.reshape(1, d))
    return out.reshape(b, s, d)


def kernel(x, g_mix, w_in, b_in, sinks, w_pool, b_pool, pool_scale, w_out, b_out, g_ffn, w_gate, w_up, w_down, g_final):
    depth = g_mix.shape[0]
    for layer in range(depth):
        x = _layer(x, g_mix[layer], w_in[layer], b_in[layer], sinks[layer], w_pool[layer],
                   b_pool[layer], pool_scale[layer], w_out[layer], b_out[layer],
                   g_ffn[layer], w_gate[layer], w_up[layer], w_down[layer], g_final,
                   final_norm=layer == depth - 1)
    return x
```

```python
import functools

import jax
import jax.numpy as jnp
from jax import lax
from jax.experimental import pallas as pl
from jax.experimental.pallas import tpu as pltpu

D_MODEL = 1024
HEAD_DIM = 64
HALF_HEAD = HEAD_DIM // 2
N_Q_HEADS = 8
N_KV_HEADS = 2
GQA_GROUP = N_Q_HEADS // N_KV_HEADS
ATTN_WIDTH = N_Q_HEADS * HEAD_DIM
KV_WIDTH = N_KV_HEADS * HEAD_DIM
POOL_SIZES = (2, 4, 8, 16)
N_POOL_GROUPS = len(POOL_SIZES)
POOL_GROUP_WIDTH = 128
POOL_WIDTH = N_POOL_GROUPS * POOL_GROUP_WIDTH
IN_WIDTH = ATTN_WIDTH + 2 * KV_WIDTH + POOL_WIDTH
WINDOW = 128
BLOCK = 128
ROPE_THETA = 10000.0
RMS_EPS = 1e-5
LOG2_E = 1.4426950408889634
POOL_HALO = 16

LANES = 128
MXU_WIDTH = 256
IN_CHUNKS = (256, 256, 256, 256, 256)
VMEM_LIMIT_BYTES = 60 * 1024 * 1024

BF16_SUBLANES = 16
ROWS = 512
WEIGHT_CHUNKS = 8


def _rmsnorm(x, g):
    ms = jnp.mean(x * x, axis=-1, keepdims=True)
    return x * lax.rsqrt(ms + RMS_EPS) * g


def _residual_projection(residual, lhs, w_ref, bias_ref=None):
    chunks, sumsq = [], None
    for c in range(w_ref.shape[1] // MXU_WIDTH):
        cols = slice(c * MXU_WIDTH, (c + 1) * MXU_WIDTH)
        y = residual(cols) + jnp.dot(lhs, w_ref[:, cols], preferred_element_type=jnp.float32)
        if bias_ref is not None:
            y = y + bias_ref[:, cols]
        part = jnp.sum(y * y, axis=-1, keepdims=True)
        sumsq = part if sumsq is None else sumsq + part
        chunks.append(y)
    return chunks, sumsq


def _rmsnorm_chunks(chunks, sumsq, g_ref):
    width = sum(y.shape[1] for y in chunks)
    r = lax.rsqrt(sumsq * (1.0 / width) + RMS_EPS)
    return [y * r * g_ref[:, c * MXU_WIDTH:(c + 1) * MXU_WIDTH] for c, y in enumerate(chunks)]


def _ffn_block(x_chunks, x_sumsq, g_ref, wg_ref, wu_ref, wd_ref, gf_ref, final_norm):
    width = sum(y.shape[1] for y in x_chunks)
    xg = jnp.concatenate([y * g_ref[:, c * MXU_WIDTH:(c + 1) * MXU_WIDTH] for c, y in enumerate(x_chunks)],
                         axis=1).astype(jnp.bfloat16)
    r = lax.rsqrt(x_sumsq * (1.0 / width) + RMS_EPS)
    gate = r * jnp.dot(xg, wg_ref[...], preferred_element_type=jnp.float32)
    up = r * jnp.dot(xg, wu_ref[...], preferred_element_type=jnp.float32)
    act = (gate * jax.nn.sigmoid(gate) * up).astype(jnp.bfloat16)
    y_chunks, y_sumsq = _residual_projection(lambda cols: x_chunks[cols.start // MXU_WIDTH], act, wd_ref)
    return _rmsnorm_chunks(y_chunks, y_sumsq, gf_ref) if final_norm else y_chunks


def _attention_block(j, seq_block, sinks_ref, q_ref, kext_ref, vext_ref):
    rows = q_ref.shape[0]
    qi = lax.broadcasted_iota(jnp.int32, (BLOCK, 2 * BLOCK), 0) + BLOCK
    sj = lax.broadcasted_iota(jnp.int32, (BLOCK, 2 * BLOCK), 1)
    delta = qi - sj
    key_pos = seq_block * rows + j * BLOCK + sj - BLOCK
    visible = (delta >= 0) & (delta < WINDOW) & (key_pos >= 0)
    bias = jnp.where(visible, 0.0, -jnp.inf).astype(jnp.float32)
    low_b = lax.broadcasted_iota(jnp.int32, (BLOCK, LANES), 1) < HEAD_DIM
    pairs_per_group = GQA_GROUP // 2

    pair_outs = []
    for g in range(N_KV_HEADS):
        kband = kext_ref[g, j * BLOCK:(j + 2) * BLOCK, :]
        vband = vext_ref[g, j * BLOCK:(j + 2) * BLOCK, :]
        qm = []
        for p in range(pairs_per_group):
            c = (g * pairs_per_group + p) * LANES
            qp = q_ref[j * BLOCK:(j + 1) * BLOCK, c:c + LANES]
            qm.append(jnp.where(low_b, qp, jnp.zeros_like(qp)))
            qm.append(jnp.where(low_b, jnp.zeros_like(qp), qp))
        qm = jnp.concatenate(qm, axis=0)
        s = lax.dot_general(qm, kband, (((1,), (1,)), ((), ())),
                            preferred_element_type=jnp.float32)
        ps, inv_ls = [], []
        for hh in range(GQA_GROUP):
            sink = sinks_ref[g * GQA_GROUP + hh] * LOG2_E
            s_h = s[hh * BLOCK:(hh + 1) * BLOCK] + bias
            m = jnp.maximum(jnp.max(s_h, axis=-1, keepdims=True), sink)
            p_h = jnp.exp2(s_h - m)
            l = jnp.sum(p_h, axis=-1, keepdims=True) + jnp.exp2(sink - m)
            ps.append(p_h.astype(jnp.bfloat16))
            inv_ls.append(1.0 / l)
        pm = jnp.concatenate(ps, axis=0)
        o = jnp.dot(pm, vband, preferred_element_type=jnp.float32)
        for p in range(pairs_per_group):
            o_a = o[(2 * p) * BLOCK:(2 * p + 1) * BLOCK] * inv_ls[2 * p]
            o_b = o[(2 * p + 1) * BLOCK:(2 * p + 2) * BLOCK] * inv_ls[2 * p + 1]
            pair_outs.append(jnp.where(low_b, o_a, o_b))
    return jnp.concatenate(pair_outs, axis=1)


def _store_projection(z, seq_block, wpool_ref, bpool_ref, pscale_ref, rcos_ref, rsin_ref, bcos_ref, bsin_ref,
                      q_ref, kext_ref, vext_ref, utail_ref, pool_ref):
    rows = z[0].shape[0]
    lane = lax.broadcasted_iota(jnp.int32, (rows, LANES), 1)
    low_half = lane < HEAD_DIM
    first_half = (lane % HEAD_DIM) < HALF_HEAD
    rcos, rsin, bcos, bsin = rcos_ref[...], rsin_ref[...], bcos_ref[seq_block], bsin_ref[seq_block]
    cos = rcos * bcos - rsin * bsin
    sin = jnp.where(first_half, -1.0, 1.0) * (rsin * bcos + rcos * bsin)

    def rope(t):
        partner = jnp.where(first_half, pltpu.roll(t, LANES - HALF_HEAD, 1), pltpu.roll(t, HALF_HEAD, 1))
        return t * cos + partner * sin

    q_pieces = ATTN_WIDTH // LANES
    scale = HEAD_DIM ** -0.5 * LOG2_E
    for p in range(q_pieces):
        q_ref[:, p * LANES:(p + 1) * LANES] = (rope(z[p]) * scale).astype(q_ref.dtype)

    keep = seq_block > 0
    for ref in (kext_ref, vext_ref):
        tail = ref[:, rows:rows + BLOCK, :]
        ref[:, 0:BLOCK, :] = jnp.where(keep, tail, jnp.zeros_like(tail))

    k = rope(z[q_pieces])
    v = z[q_pieces + 1]
    k_sw = pltpu.roll(k, HEAD_DIM, 1)
    v_sw = pltpu.roll(v, HEAD_DIM, 1)
    kext_ref[0, BLOCK:, :] = jnp.where(low_half, k, k_sw).astype(kext_ref.dtype)
    kext_ref[1, BLOCK:, :] = jnp.where(low_half, k_sw, k).astype(kext_ref.dtype)
    vext_ref[0, BLOCK:, :] = jnp.where(low_half, v, v_sw).astype(vext_ref.dtype)
    vext_ref[1, BLOCK:, :] = jnp.where(low_half, v_sw, v).astype(vext_ref.dtype)

    halo = utail_ref[...]
    halo = jnp.where(keep, halo, jnp.zeros_like(halo))
    pos = seq_block * rows + lax.broadcasted_iota(jnp.int32, (rows, LANES), 0)
    for gi, size in enumerate(POOL_SIZES):
        cols = slice(gi * LANES, (gi + 1) * LANES)
        u_g = z[q_pieces + 2 + gi]
        utail_ref[:, cols] = u_g[rows - POOL_HALO:]
        acc = jnp.concatenate([halo[:, cols], u_g], axis=0)
        span = 1
        while span < size:
            acc = acc + pltpu.roll(acc, span, 0)
            span *= 2
        count = jnp.minimum(pos + 1, size).astype(jnp.float32)
        mixed = acc[POOL_HALO:] / count - u_g
        y = jnp.dot(mixed.astype(jnp.bfloat16), wpool_ref[gi].astype(jnp.bfloat16),
                    preferred_element_type=jnp.float32)
        pool_ref[:, cols] = ((y + bpool_ref[:, cols]) * pscale_ref[:, cols]).astype(pool_ref.dtype)


def _layer_step(i, sinks_ref, xprev_ref, xnext_ref, gmix_ref, bin_ref, wpool_ref, bpool_ref, pscale_ref,
                bout_ref, rcos_ref, rsin_ref, bcos_ref, bsin_ref, gffn_ref, gf_ref, o_ref, win_ref, wout_ref, wg_ref, wu_ref, wd_ref,
                h_ref, q_ref, kext_ref, vext_ref, utail_ref, pool_ref, *, seq_blocks, final_norm):
    rows = o_ref.shape[0]
    chunk_starts = [sum(IN_CHUNKS[:t]) for t in range(len(IN_CHUNKS))]
    n_chunks = len(IN_CHUNKS)
    n_attn = rows // BLOCK
    project_refs = (wpool_ref, bpool_ref, pscale_ref, rcos_ref, rsin_ref, bcos_ref, bsin_ref,
                    q_ref, kext_ref, vext_ref, utail_ref, pool_ref)

    def in_projection_chunk(h, t):
        cols = slice(chunk_starts[t], chunk_starts[t] + IN_CHUNKS[t])
        half = h.shape[0] // 2
        zc = jnp.concatenate([jnp.dot(h[:half], win_ref[:, cols], preferred_element_type=jnp.float32),
                              jnp.dot(h[half:], win_ref[:, cols], preferred_element_type=jnp.float32)],
                             axis=0) + bin_ref[:, cols]
        return [zc[:, c * LANES:(c + 1) * LANES] for c in range(IN_CHUNKS[t] // LANES)]

    @pl.when(i == 0)
    def _():
        kext_ref[...] = jnp.zeros(kext_ref.shape, kext_ref.dtype)
        vext_ref[...] = jnp.zeros(vext_ref.shape, vext_ref.dtype)
        utail_ref[...] = jnp.zeros(utail_ref.shape, utail_ref.dtype)
        h = _rmsnorm(xprev_ref[...], gmix_ref[...]).astype(h_ref.dtype)
        z = sum((in_projection_chunk(h, t) for t in range(n_chunks)), [])
        _store_projection(z, 0, *project_refs)
        h_ref[...] = _rmsnorm(xnext_ref[...], gmix_ref[...]).astype(h_ref.dtype)

    @pl.when(i > 0)
    def _():
        prev_seq_block = (i + seq_blocks - 1) % seq_blocks
        h = h_ref[...]
        z, attn = [], []
        for t in range(max(n_chunks, n_attn)):
            if t < n_attn:
                attn.append(_attention_block(t, prev_seq_block, sinks_ref, q_ref, kext_ref, vext_ref))
            if t < n_chunks:
                z += in_projection_chunk(h, t)

        cat = jnp.concatenate([jnp.concatenate(attn, axis=0).astype(jnp.bfloat16), pool_ref[...]], axis=1)
        x1_chunks, x1_sumsq = _residual_projection(lambda cols: xprev_ref[:, cols], cat, wout_ref, bout_ref)

        _store_projection(z, i % seq_blocks, *project_refs)

        h_ref[...] = _rmsnorm(xnext_ref[...], gmix_ref[...]).astype(h_ref.dtype)

        out_chunks = _ffn_block(x1_chunks, x1_sumsq, gffn_ref, wg_ref, wu_ref, wd_ref, gf_ref, final_norm)
        for c, y in enumerate(out_chunks):
            o_ref[:, c * MXU_WIDTH:(c + 1) * MXU_WIDTH] = y


def _layer_kernel(sinks_ref, xprev_ref, xnext_ref, gmix_ref, win32_ref, bin_ref, wpool_ref, bpool_ref,
                  pscale_ref, wout32_ref, bout_ref, rcos_ref, rsin_ref, bcos_ref, bsin_ref, gffn_ref, wg32_ref, wu32_ref, wd32_ref,
                  gf_ref, o_ref, win_ref, wout_ref, wg_ref, wu_ref, wd_ref, *scratch, w_chunks, **static):
    i = pl.program_id(0)

    @pl.when(i < w_chunks)
    def _():
        for src, dst in ((win32_ref, win_ref), (wout32_ref, wout_ref), (wg32_ref, wg_ref),
                         (wu32_ref, wu_ref), (wd32_ref, wd_ref)):
            chunk = src.shape[0]
            dst[pl.ds(pl.multiple_of(i * chunk, chunk), chunk), :] = src[...].astype(dst.dtype)

    @pl.when(i >= w_chunks)
    def _():
        _layer_step(i - w_chunks, sinks_ref, xprev_ref, xnext_ref, gmix_ref, bin_ref, wpool_ref, bpool_ref,
                    pscale_ref, bout_ref, rcos_ref, rsin_ref, bcos_ref, bsin_ref, gffn_ref, gf_ref, o_ref, win_ref, wout_ref,
                    wg_ref, wu_ref, wd_ref, *scratch, **static)


def _resident(shape):
    zeros = (0,) * len(shape)
    return pl.BlockSpec(shape, lambda *_: zeros, pipeline_mode=pl.Buffered(1))


def _rope_tables(rows, seq_blocks):
    inv_freq = 1.0 / (ROPE_THETA ** (jnp.arange(0, HEAD_DIM, 2, dtype=jnp.float32) / HEAD_DIM))
    freq = jnp.tile(inv_freq, LANES // HALF_HEAD)[None, :]
    row_ang = jnp.arange(rows, dtype=jnp.float32)[:, None] * freq
    start_ang = (jnp.arange(seq_blocks, dtype=jnp.float32) * rows)[:, None, None] * freq
    return jnp.cos(row_ang), jnp.sin(row_ang), jnp.cos(start_ang), jnp.sin(start_ang)


def _layer(x, g_mix, w_in, b_in, sinks, w_pool, b_pool, pool_scale, w_out, b_out,
           g_ffn, w_gate, w_up, w_down, g_final, *, final_norm):
    b, s, d = x.shape
    rows = ROWS
    assert s % rows == 0 and rows % BLOCK == 0 and d == D_MODEL and sum(IN_CHUNKS) == IN_WIDTH
    seq_blocks = s // rows
    n = b * seq_blocks
    row_cos, row_sin, start_cos, start_sin = _rope_tables(rows, seq_blocks)
    bf16 = jnp.bfloat16
    lead = WEIGHT_CHUNKS

    def prev(i, *_):
        return (jnp.clip(i - lead - 1, 0, n - 1), 0)

    def nxt(i, *_):
        return (jnp.clip(i - lead + 1, 0, n - 1), 0)

    def weight_chunk(w):
        chunk = w.shape[0] // lead
        assert chunk * lead == w.shape[0] and chunk % BF16_SUBLANES == 0
        return pl.BlockSpec((chunk, w.shape[1]), lambda i, *_: (jnp.minimum(i, lead - 1), 0))

    x2d = x.reshape(b * s, d)
    weights = (w_in, w_out, w_gate, w_up, w_down)
    grid_spec = pltpu.PrefetchScalarGridSpec(
        num_scalar_prefetch=1,
        grid=(lead + n + 1,),
        in_specs=[
            pl.BlockSpec((rows, d), prev),
            pl.BlockSpec((rows, d), nxt),
            _resident((1, d)),
            weight_chunk(w_in),
            _resident((1, IN_WIDTH)),
            _resident((N_POOL_GROUPS, POOL_GROUP_WIDTH, POOL_GROUP_WIDTH)),
            _resident((1, POOL_WIDTH)),
            _resident((1, POOL_WIDTH)),
            weight_chunk(w_out),
            _resident((1, d)),
            _resident((rows, LANES)),
            _resident((rows, LANES)),
            _resident((seq_blocks, 1, LANES)),
            _resident((seq_blocks, 1, LANES)),
            _resident((1, d)),
            weight_chunk(w_gate),
            weight_chunk(w_up),
            weight_chunk(w_down),
            _resident((1, d)),
        ],
        out_specs=pl.BlockSpec((rows, d), prev),
        scratch_shapes=[pltpu.VMEM(w.shape, bf16) for w in weights] + [
            pltpu.VMEM((rows, d), bf16),
            pltpu.VMEM((rows, ATTN_WIDTH), bf16),
            pltpu.VMEM((N_KV_HEADS, rows + BLOCK, LANES), bf16),
            pltpu.VMEM((N_KV_HEADS, rows + BLOCK, LANES), bf16),
            pltpu.VMEM((POOL_HALO, POOL_WIDTH), jnp.float32),
            pltpu.VMEM((rows, POOL_WIDTH), bf16),
        ],
    )
    out = pl.pallas_call(
        functools.partial(_layer_kernel, w_chunks=lead, seq_blocks=seq_blocks, final_norm=final_norm),
        grid_spec=grid_spec,
        out_shape=jax.ShapeDtypeStruct((b * s, d), x.dtype),
        compiler_params=pltpu.CompilerParams(
            dimension_semantics=("arbitrary",),
            vmem_limit_bytes=VMEM_LIMIT_BYTES),
        name="layer",
    )(sinks.astype(jnp.float32), x2d, x2d, g_mix.reshape(1, d), w_in,
      b_in.reshape(1, IN_WIDTH), w_pool, b_pool.reshape(1, POOL_WIDTH),
      pool_scale.reshape(1, POOL_WIDTH), w_out, b_out.reshape(1, d),
      row_cos, row_sin, start_cos, start_sin,
      g_ffn.reshape(1, d), w_gate, w_up, w_down, g_final.reshape(1, d))
    return out.reshape(b, s, d)


def kernel(x, g_mix, w_in, b_in, sinks, w_pool, b_pool, pool_scale, w_out, b_out, g_ffn, w_gate, w_up, w_down, g_final):
    depth = g_mix.shape[0]
    for layer in range(depth):
        x = _layer(x, g_mix[layer], w_in[layer], b_in[layer], sinks[layer], w_pool[layer],
                   b_pool[layer], pool_scale[layer], w_out[layer], b_out[layer],
                   g_ffn[layer], w_gate[layer], w_up[layer], w_down[layer], g_final,
                   final_norm=layer == depth - 1)
    return x
```

```python
import functools

import jax
import jax.numpy as jnp
from jax import lax
from jax.experimental import pallas as pl
from jax.experimental.pallas import tpu as pltpu

D_MODEL = 1024
HEAD_DIM = 64
HALF_HEAD = HEAD_DIM // 2
N_Q_HEADS = 8
N_KV_HEADS = 2
GQA_GROUP = N_Q_HEADS // N_KV_HEADS
ATTN_WIDTH = N_Q_HEADS * HEAD_DIM
KV_WIDTH = N_KV_HEADS * HEAD_DIM
POOL_SIZES = (2, 4, 8, 16)
N_POOL_GROUPS = len(POOL_SIZES)
POOL_GROUP_WIDTH = 128
POOL_WIDTH = N_POOL_GROUPS * POOL_GROUP_WIDTH
IN_WIDTH = ATTN_WIDTH + 2 * KV_WIDTH + POOL_WIDTH
WINDOW = 128
BLOCK = 128
ROPE_THETA = 10000.0
RMS_EPS = 1e-5
LOG2_E = 1.4426950408889634
POOL_HALO = 16

LANES = 128
MXU_WIDTH = 256
IN_CHUNKS = (256, 256, 256, 256, 256)
VMEM_LIMIT_BYTES = 60 * 1024 * 1024

BF16_SUBLANES = 16
ROWS = 512
WEIGHT_CHUNKS = 8
X_SLOTS = 4
OUT_SLOTS = 2


def _rmsnorm(x, g):
    ms = jnp.mean(x * x, axis=-1, keepdims=True)
    return x * lax.rsqrt(ms + RMS_EPS) * g


def _residual_projection(residual, lhs, w_ref, bias_ref=None):
    chunks, sumsq = [], None
    for c in range(w_ref.shape[1] // MXU_WIDTH):
        cols = slice(c * MXU_WIDTH, (c + 1) * MXU_WIDTH)
        y = residual(cols) + jnp.dot(lhs, w_ref[:, cols], preferred_element_type=jnp.float32)
        if bias_ref is not None:
            y = y + bias_ref[:, cols]
        part = jnp.sum(y * y, axis=-1, keepdims=True)
        sumsq = part if sumsq is None else sumsq + part
        chunks.append(y)
    return chunks, sumsq


def _rmsnorm_chunks(chunks, sumsq, g_ref):
    width = sum(y.shape[1] for y in chunks)
    r = lax.rsqrt(sumsq * (1.0 / width) + RMS_EPS)
    return [y * r * g_ref[:, c * MXU_WIDTH:(c + 1) * MXU_WIDTH] for c, y in enumerate(chunks)]


def _ffn_block(x_chunks, x_sumsq, g_ref, wg_ref, wu_ref, wd_ref, gf_ref, final_norm):
    width = sum(y.shape[1] for y in x_chunks)
    xg = jnp.concatenate([y * g_ref[:, c * MXU_WIDTH:(c + 1) * MXU_WIDTH] for c, y in enumerate(x_chunks)],
                         axis=1).astype(jnp.bfloat16)
    r = lax.rsqrt(x_sumsq * (1.0 / width) + RMS_EPS)
    gate = r * jnp.dot(xg, wg_ref[...], preferred_element_type=jnp.float32)
    up = r * jnp.dot(xg, wu_ref[...], preferred_element_type=jnp.float32)
    act = (gate * jax.nn.sigmoid(gate) * up).astype(jnp.bfloat16)
    y_chunks, y_sumsq = _residual_projection(lambda cols: x_chunks[cols.start // MXU_WIDTH], act, wd_ref)
    return _rmsnorm_chunks(y_chunks, y_sumsq, gf_ref) if final_norm else y_chunks


def _attention_block(j, seq_block, sinks_ref, q_ref, kext_ref, vext_ref):
    rows = q_ref.shape[0]
    qi = lax.broadcasted_iota(jnp.int32, (BLOCK, 2 * BLOCK), 0) + BLOCK
    sj = lax.broadcasted_iota(jnp.int32, (BLOCK, 2 * BLOCK), 1)
    delta = qi - sj
    key_pos = seq_block * rows + j * BLOCK + sj - BLOCK
    visible = (delta >= 0) & (delta < WINDOW) & (key_pos >= 0)
    bias = jnp.where(visible, 0.0, -jnp.inf).astype(jnp.float32)
    low_b = lax.broadcasted_iota(jnp.int32, (BLOCK, LANES), 1) < HEAD_DIM
    pairs_per_group = GQA_GROUP // 2

    pair_outs = []
    for g in range(N_KV_HEADS):
        kband = kext_ref[g, j * BLOCK:(j + 2) * BLOCK, :]
        vband = vext_ref[g, j * BLOCK:(j + 2) * BLOCK, :]
        qm = []
        for p in range(pairs_per_group):
            c = (g * pairs_per_group + p) * LANES
            qp = q_ref[j * BLOCK:(j + 1) * BLOCK, c:c + LANES]
            qm.append(jnp.where(low_b, qp, jnp.zeros_like(qp)))
            qm.append(jnp.where(low_b, jnp.zeros_like(qp), qp))
        qm = jnp.concatenate(qm, axis=0)
        s = lax.dot_general(qm, kband, (((1,), (1,)), ((), ())),
                            preferred_element_type=jnp.float32)
        ps, inv_ls = [], []
        for hh in range(GQA_GROUP):
            sink = sinks_ref[g * GQA_GROUP + hh] * LOG2_E
            s_h = s[hh * BLOCK:(hh + 1) * BLOCK] + bias
            m = jnp.maximum(jnp.max(s_h, axis=-1, keepdims=True), sink)
            p_h = jnp.exp2(s_h - m)
            l = jnp.sum(p_h, axis=-1, keepdims=True) + jnp.exp2(sink - m)
            ps.append(p_h.astype(jnp.bfloat16))
            inv_ls.append(1.0 / l)
        pm = jnp.concatenate(ps, axis=0)
        o = jnp.dot(pm, vband, preferred_element_type=jnp.float32)
        for p in range(pairs_per_group):
            o_a = o[(2 * p) * BLOCK:(2 * p + 1) * BLOCK] * inv_ls[2 * p]
            o_b = o[(2 * p + 1) * BLOCK:(2 * p + 2) * BLOCK] * inv_ls[2 * p + 1]
            pair_outs.append(jnp.where(low_b, o_a, o_b))
    return jnp.concatenate(pair_outs, axis=1)


def _store_projection(z, seq_block, wpool_ref, bpool_ref, pscale_ref, rcos_ref, rsin_ref, bcos_ref, bsin_ref,
                      q_ref, kext_ref, vext_ref, utail_ref, pool_ref):
    rows = z[0].shape[0]
    lane = lax.broadcasted_iota(jnp.int32, (rows, LANES), 1)
    low_half = lane < HEAD_DIM
    first_half = (lane % HEAD_DIM) < HALF_HEAD
    rcos, rsin, bcos, bsin = rcos_ref[...], rsin_ref[...], bcos_ref[seq_block], bsin_ref[seq_block]
    cos = rcos * bcos - rsin * bsin
    sin = jnp.where(first_half, -1.0, 1.0) * (rsin * bcos + rcos * bsin)

    def rope(t):
        partner = jnp.where(first_half, pltpu.roll(t, LANES - HALF_HEAD, 1), pltpu.roll(t, HALF_HEAD, 1))
        return t * cos + partner * sin

    q_pieces = ATTN_WIDTH // LANES
    scale = HEAD_DIM ** -0.5 * LOG2_E
    for p in range(q_pieces):
        q_ref[:, p * LANES:(p + 1) * LANES] = (rope(z[p]) * scale).astype(q_ref.dtype)

    keep = seq_block > 0
    for ref in (kext_ref, vext_ref):
        tail = ref[:, rows:rows + BLOCK, :]
        ref[:, 0:BLOCK, :] = jnp.where(keep, tail, jnp.zeros_like(tail))

    k = rope(z[q_pieces])
    v = z[q_pieces + 1]
    k_sw = pltpu.roll(k, HEAD_DIM, 1)
    v_sw = pltpu.roll(v, HEAD_DIM, 1)
    kext_ref[0, BLOCK:, :] = jnp.where(low_half, k, k_sw).astype(kext_ref.dtype)
    kext_ref[1, BLOCK:, :] = jnp.where(low_half, k_sw, k).astype(kext_ref.dtype)
    vext_ref[0, BLOCK:, :] = jnp.where(low_half, v, v_sw).astype(vext_ref.dtype)
    vext_ref[1, BLOCK:, :] = jnp.where(low_half, v_sw, v).astype(vext_ref.dtype)

    halo = utail_ref[...]
    halo = jnp.where(keep, halo, jnp.zeros_like(halo))
    pos = seq_block * rows + lax.broadcasted_iota(jnp.int32, (rows, LANES), 0)
    for gi, size in enumerate(POOL_SIZES):
        cols = slice(gi * LANES, (gi + 1) * LANES)
        u_g = z[q_pieces + 2 + gi]
        utail_ref[:, cols] = u_g[rows - POOL_HALO:]
        acc = jnp.concatenate([halo[:, cols], u_g], axis=0)
        span = 1
        while span < size:
            acc = acc + pltpu.roll(acc, span, 0)
            span *= 2
        count = jnp.minimum(pos + 1, size).astype(jnp.float32)
        mixed = acc[POOL_HALO:] / count - u_g
        y = jnp.dot(mixed.astype(jnp.bfloat16), wpool_ref[gi].astype(jnp.bfloat16),
                    preferred_element_type=jnp.float32)
        pool_ref[:, cols] = ((y + bpool_ref[:, cols]) * pscale_ref[:, cols]).astype(pool_ref.dtype)


def _layer_kernel(sinks_ref, x_hbm, gmix_ref, win32_hbm, bin_ref, wpool_ref, bpool_ref, pscale_ref,
                  wout32_hbm, bout_ref, rcos_ref, rsin_ref, bcos_ref, bsin_ref, gffn_ref, wg32_hbm, wu32_hbm,
                  wd32_hbm, gf_ref, o_hbm,
                  win_ref, wout_ref, wg_ref, wu_ref, wd_ref, win_st, wout_st, wg_st, wu_st, wd_st,
                  x_ring, out_buf, h_ref, q_ref, kext_ref, vext_ref, utail_ref, pool_ref,
                  w_sems, x_sems, o_sems, *, n_blocks, seq_blocks, final_norm):
    rows = x_ring.shape[1]
    chunk_starts = [sum(IN_CHUNKS[:t]) for t in range(len(IN_CHUNKS))]
    n_chunks = len(IN_CHUNKS)
    n_attn = rows // BLOCK

    def x_copy(block, slot):
        start = pl.multiple_of(block * rows, rows)
        return pltpu.make_async_copy(x_hbm.at[pl.ds(start, rows)], x_ring.at[slot], x_sems.at[slot])

    def out_copy(block, slot):
        start = pl.multiple_of(block * rows, rows)
        return pltpu.make_async_copy(out_buf.at[slot], o_hbm.at[pl.ds(start, rows)], o_sems.at[slot])

    weights = ((win32_hbm, win_st, win_ref), (wout32_hbm, wout_st, wout_ref), (wg32_hbm, wg_st, wg_ref),
               (wu32_hbm, wu_st, wu_ref), (wd32_hbm, wd_st, wd_ref))

    def weight_copies(c, slot):
        return [pltpu.make_async_copy(src.at[pl.ds(c * stage.shape[1], stage.shape[1])], stage.at[slot],
                                      w_sems.at[slot, k])
                for k, (src, stage, _) in enumerate(weights)]

    for b in range(X_SLOTS - 1):
        x_copy(b, b).start()
    for cp in weight_copies(0, 0):
        cp.start()
    for c in range(WEIGHT_CHUNKS):
        slot = c % 2
        if c + 1 < WEIGHT_CHUNKS:
            for cp in weight_copies(c + 1, 1 - slot):
                cp.start()
        for cp, (_, stage, dst) in zip(weight_copies(c, slot), weights):
            cp.wait()
            chunk = stage.shape[1]
            dst[c * chunk:(c + 1) * chunk, :] = stage[slot].astype(dst.dtype)

    project_refs = (wpool_ref, bpool_ref, pscale_ref, rcos_ref, rsin_ref, bcos_ref, bsin_ref,
                    q_ref, kext_ref, vext_ref, utail_ref, pool_ref)

    def in_projection_chunk(h, t):
        cols = slice(chunk_starts[t], chunk_starts[t] + IN_CHUNKS[t])
        half = h.shape[0] // 2
        zc = jnp.concatenate([jnp.dot(h[:half], win_ref[:, cols], preferred_element_type=jnp.float32),
                              jnp.dot(h[half:], win_ref[:, cols], preferred_element_type=jnp.float32)],
                             axis=0) + bin_ref[:, cols]
        return [zc[:, c * LANES:(c + 1) * LANES] for c in range(IN_CHUNKS[t] // LANES)]

    kext_ref[...] = jnp.zeros(kext_ref.shape, kext_ref.dtype)
    vext_ref[...] = jnp.zeros(vext_ref.shape, vext_ref.dtype)
    utail_ref[...] = jnp.zeros(utail_ref.shape, utail_ref.dtype)
    x_copy(0, 0).wait()
    h0 = _rmsnorm(x_ring[0], gmix_ref[...]).astype(h_ref.dtype)
    z0 = sum((in_projection_chunk(h0, t) for t in range(n_chunks)), [])
    _store_projection(z0, 0, *project_refs)
    x_copy(1, 1).wait()
    h_ref[...] = _rmsnorm(x_ring[1], gmix_ref[...]).astype(h_ref.dtype)

    def body(i, carry):
        out_slot = i % OUT_SLOTS

        @pl.when(i + 2 <= n_blocks - 1)
        def _():
            x_copy(i + 2, (i + 2) % X_SLOTS).start()

        @pl.when(i + 1 <= n_blocks - 1)
        def _():
            x_copy(i + 1, (i + 1) % X_SLOTS).wait()

        @pl.when(i > OUT_SLOTS)
        def _():
            out_copy(i - 1 - OUT_SLOTS, out_slot).wait()

        prev_seq_block = (i + seq_blocks - 1) % seq_blocks
        x_prev = x_ring.at[(i + X_SLOTS - 1) % X_SLOTS]
        h = h_ref[...]
        z, attn = [], []
        for t in range(max(n_chunks, n_attn)):
            if t < n_attn:
                attn.append(_attention_block(t, prev_seq_block, sinks_ref, q_ref, kext_ref, vext_ref))
            if t < n_chunks:
                z += in_projection_chunk(h, t)

        cat = jnp.concatenate([jnp.concatenate(attn, axis=0).astype(jnp.bfloat16), pool_ref[...]], axis=1)
        x1_chunks, x1_sumsq = _residual_projection(lambda cols: x_prev[:, cols], cat, wout_ref, bout_ref)

        _store_projection(z, i % seq_blocks, *project_refs)

        h_ref[...] = _rmsnorm(x_ring[(i + 1) % X_SLOTS], gmix_ref[...]).astype(h_ref.dtype)

        out_chunks = _ffn_block(x1_chunks, x1_sumsq, gffn_ref, wg_ref, wu_ref, wd_ref, gf_ref, final_norm)
        for c, y in enumerate(out_chunks):
            out_buf[out_slot, :, c * MXU_WIDTH:(c + 1) * MXU_WIDTH] = y

        out_copy(i - 1, out_slot).start()
        return carry

    lax.fori_loop(1, n_blocks + 1, body, 0)

    for i in range(n_blocks + 1 - OUT_SLOTS, n_blocks + 1):
        out_copy(i - 1, i % OUT_SLOTS).wait()


def _rope_tables(rows, seq_blocks):
    inv_freq = 1.0 / (ROPE_THETA ** (jnp.arange(0, HEAD_DIM, 2, dtype=jnp.float32) / HEAD_DIM))
    freq = jnp.tile(inv_freq, LANES // HALF_HEAD)[None, :]
    row_ang = jnp.arange(rows, dtype=jnp.float32)[:, None] * freq
    start_ang = (jnp.arange(seq_blocks, dtype=jnp.float32) * rows)[:, None, None] * freq
    return jnp.cos(row_ang), jnp.sin(row_ang), jnp.cos(start_ang), jnp.sin(start_ang)


def _layer(x, g_mix, w_in, b_in, sinks, w_pool, b_pool, pool_scale, w_out, b_out,
           g_ffn, w_gate, w_up, w_down, g_final, *, final_norm):
    b, s, d = x.shape
    rows = ROWS
    assert s % rows == 0 and rows % BLOCK == 0 and d == D_MODEL and sum(IN_CHUNKS) == IN_WIDTH
    seq_blocks = s // rows
    n = b * seq_blocks
    assert n >= X_SLOTS
    row_cos, row_sin, start_cos, start_sin = _rope_tables(rows, seq_blocks)
    bf16 = jnp.bfloat16
    weights = (w_in, w_out, w_gate, w_up, w_down)
    for w in weights:
        assert w.shape[0] % WEIGHT_CHUNKS == 0 and (w.shape[0] // WEIGHT_CHUNKS) % BF16_SUBLANES == 0

    vmem = pl.BlockSpec(memory_space=pltpu.VMEM)
    hbm = pl.BlockSpec(memory_space=pl.ANY)
    smem = pl.BlockSpec(memory_space=pltpu.SMEM)
    out = pl.pallas_call(
        functools.partial(_layer_kernel, n_blocks=n, seq_blocks=seq_blocks, final_norm=final_norm),
        in_specs=[smem, hbm, vmem, hbm, vmem, vmem, vmem, vmem, hbm, vmem, vmem, vmem, vmem, vmem, vmem,
                  hbm, hbm, hbm, vmem],
        out_specs=hbm,
        out_shape=jax.ShapeDtypeStruct((b * s, d), x.dtype),
        scratch_shapes=(
            [pltpu.VMEM(w.shape, bf16) for w in weights]
            + [pltpu.VMEM((2, w.shape[0] // WEIGHT_CHUNKS, w.shape[1]), jnp.float32) for w in weights]
            + [
                pltpu.VMEM((X_SLOTS, rows, d), jnp.float32),
                pltpu.VMEM((OUT_SLOTS, rows, d), jnp.float32),
                pltpu.VMEM((rows, d), bf16),
                pltpu.VMEM((rows, ATTN_WIDTH), bf16),
                pltpu.VMEM((N_KV_HEADS, rows + BLOCK, LANES), bf16),
                pltpu.VMEM((N_KV_HEADS, rows + BLOCK, LANES), bf16),
                pltpu.VMEM((POOL_HALO, POOL_WIDTH), jnp.float32),
                pltpu.VMEM((rows, POOL_WIDTH), bf16),
                pltpu.SemaphoreType.DMA((2, len(weights))),
                pltpu.SemaphoreType.DMA((X_SLOTS,)),
                pltpu.SemaphoreType.DMA((OUT_SLOTS,)),
            ]),
        compiler_params=pltpu.CompilerParams(vmem_limit_bytes=VMEM_LIMIT_BYTES),
        name="layer",
    )(sinks.astype(jnp.float32), x.reshape(b * s, d), g_mix.reshape(1, d), w_in,
      b_in.reshape(1, IN_WIDTH), w_pool, b_pool.reshape(1, POOL_WIDTH),
      pool_scale.reshape(1, POOL_WIDTH), w_out, b_out.reshape(1, d),
      row_cos, row_sin, start_cos, start_sin,
      g_ffn.reshape(1, d), w_gate, w_up, w_down, g_final.reshape(1, d))
    return out.reshape(b, s, d)


def kernel(x, g_mix, w_in, b_in, sinks, w_pool, b_pool, pool_scale, w_out, b_out, g_ffn, w_gate, w_up, w_down, g_final):
    depth = g_mix.shape[0]
    for layer in range(depth):
        x = _layer(x, g_mix[layer], w_in[layer], b_in[layer], sinks[layer], w_pool[layer],
                   b_pool[layer], pool_scale[layer], w_out[layer], b_out[layer],
                   g_ffn[layer], w_gate[layer], w_up[layer], w_down[layer], g_final,
                   final_norm=layer == depth - 1)
    return x
```

```python
import functools

import jax
import jax.numpy as jnp
from jax import lax
from jax.experimental import pallas as pl
from jax.experimental.pallas import tpu as pltpu

D_MODEL = 1024
HEAD_DIM = 64
HALF_HEAD = HEAD_DIM // 2
N_Q_HEADS = 8
N_KV_HEADS = 2
GQA_GROUP = N_Q_HEADS // N_KV_HEADS
ATTN_WIDTH = N_Q_HEADS * HEAD_DIM
KV_WIDTH = N_KV_HEADS * HEAD_DIM
POOL_SIZES = (2, 4, 8, 16)
N_POOL_GROUPS = len(POOL_SIZES)
POOL_GROUP_WIDTH = 128
POOL_WIDTH = N_POOL_GROUPS * POOL_GROUP_WIDTH
IN_WIDTH = ATTN_WIDTH + 2 * KV_WIDTH + POOL_WIDTH
WINDOW = 128
BLOCK = 128
ROPE_THETA = 10000.0
RMS_EPS = 1e-5
LOG2_E = 1.4426950408889634
POOL_HALO = 16

LANES = 128
MXU_WIDTH = 256
IN_CHUNKS = (256, 256, 256, 256, 256)
VMEM_LIMIT_BYTES = 60 * 1024 * 1024

BF16_SUBLANES = 16
ROWS = 512
WEIGHT_CHUNKS = 8
X_SLOTS = 4
OUT_SLOTS = 2


def _rmsnorm(x, g):
    ms = jnp.mean(x * x, axis=-1, keepdims=True)
    return x * lax.rsqrt(ms + RMS_EPS) * g


def _residual_projection(residual, lhs, w_ref, bias_ref=None):
    chunks, sumsq = [], None
    for c in range(w_ref.shape[1] // MXU_WIDTH):
        cols = slice(c * MXU_WIDTH, (c + 1) * MXU_WIDTH)
        y = residual(cols) + jnp.dot(lhs, w_ref[:, cols], preferred_element_type=jnp.float32)
        if bias_ref is not None:
            y = y + bias_ref[:, cols]
        part = jnp.sum(y * y, axis=-1, keepdims=True)
        sumsq = part if sumsq is None else sumsq + part
        chunks.append(y)
    return chunks, sumsq


def _rmsnorm_chunks(chunks, sumsq, g_ref):
    width = sum(y.shape[1] for y in chunks)
    r = lax.rsqrt(sumsq * (1.0 / width) + RMS_EPS)
    return [y * r * g_ref[:, c * MXU_WIDTH:(c + 1) * MXU_WIDTH] for c, y in enumerate(chunks)]


def _ffn_block(x_chunks, x_sumsq, g_ref, wg_ref, wu_ref, wd_ref, gf_ref, final_norm):
    width = sum(y.shape[1] for y in x_chunks)
    xg = jnp.concatenate([y * g_ref[:, c * MXU_WIDTH:(c + 1) * MXU_WIDTH] for c, y in enumerate(x_chunks)],
                         axis=1).astype(jnp.bfloat16)
    r = lax.rsqrt(x_sumsq * (1.0 / width) + RMS_EPS)
    gate = r * jnp.dot(xg, wg_ref[...], preferred_element_type=jnp.float32)
    up = r * jnp.dot(xg, wu_ref[...], preferred_element_type=jnp.float32)
    act = (gate * jax.nn.sigmoid(gate) * up).astype(jnp.bfloat16)
    y_chunks, y_sumsq = _residual_projection(lambda cols: x_chunks[cols.start // MXU_WIDTH], act, wd_ref)
    return _rmsnorm_chunks(y_chunks, y_sumsq, gf_ref) if final_norm else y_chunks


def _attention_block(j, seq_block, sinks_ref, q_ref, kext_ref, vext_ref):
    rows = q_ref.shape[0]
    qi = lax.broadcasted_iota(jnp.int32, (BLOCK, 2 * BLOCK), 0) + BLOCK
    sj = lax.broadcasted_iota(jnp.int32, (BLOCK, 2 * BLOCK), 1)
    delta = qi - sj
    key_pos = seq_block * rows + j * BLOCK + sj - BLOCK
    visible = (delta >= 0) & (delta < WINDOW) & (key_pos >= 0)
    bias = jnp.where(visible, 0.0, -jnp.inf).astype(jnp.float32)
    low_b = lax.broadcasted_iota(jnp.int32, (BLOCK, LANES), 1) < HEAD_DIM
    pairs_per_group = GQA_GROUP // 2

    pair_outs = []
    for g in range(N_KV_HEADS):
        kband = kext_ref[g, j * BLOCK:(j + 2) * BLOCK, :]
        vband = vext_ref[g, j * BLOCK:(j + 2) * BLOCK, :]
        qm = []
        for p in range(pairs_per_group):
            c = (g * pairs_per_group + p) * LANES
            qp = q_ref[j * BLOCK:(j + 1) * BLOCK, c:c + LANES]
            qm.append(jnp.where(low_b, qp, jnp.zeros_like(qp)))
            qm.append(jnp.where(low_b, jnp.zeros_like(qp), qp))
        qm = jnp.concatenate(qm, axis=0)
        s = lax.dot_general(qm, kband, (((1,), (1,)), ((), ())),
                            preferred_element_type=jnp.float32)
        ps, inv_ls = [], []
        for hh in range(GQA_GROUP):
            sink = sinks_ref[g * GQA_GROUP + hh] * LOG2_E
            s_h = s[hh * BLOCK:(hh + 1) * BLOCK] + bias
            m = jnp.maximum(jnp.max(s_h, axis=-1, keepdims=True), sink)
            p_h = jnp.exp2(s_h - m)
            l = jnp.sum(p_h, axis=-1, keepdims=True) + jnp.exp2(sink - m)
            ps.append(p_h.astype(jnp.bfloat16))
            inv_ls.append(1.0 / l)
        pm = jnp.concatenate(ps, axis=0)
        o = jnp.dot(pm, vband, preferred_element_type=jnp.float32)
        for p in range(pairs_per_group):
            o_a = o[(2 * p) * BLOCK:(2 * p + 1) * BLOCK] * inv_ls[2 * p]
            o_b = o[(2 * p + 1) * BLOCK:(2 * p + 2) * BLOCK] * inv_ls[2 * p + 1]
            pair_outs.append(jnp.where(low_b, o_a, o_b))
    return jnp.concatenate(pair_outs, axis=1)


def _store_projection(z, seq_block, wpool_ref, bpool_ref, pscale_ref, rcos_ref, rsin_ref, bcos_ref, bsin_ref,
                      q_ref, kext_ref, vext_ref, utail_ref, pool_ref):
    rows = z[0].shape[0]
    lane = lax.broadcasted_iota(jnp.int32, (rows, LANES), 1)
    low_half = lane < HEAD_DIM
    first_half = (lane % HEAD_DIM) < HALF_HEAD
    rcos, rsin, bcos, bsin = rcos_ref[...], rsin_ref[...], bcos_ref[seq_block], bsin_ref[seq_block]
    cos = rcos * bcos - rsin * bsin
    sin = jnp.where(first_half, -1.0, 1.0) * (rsin * bcos + rcos * bsin)

    def rope(t):
        partner = jnp.where(first_half, pltpu.roll(t, LANES - HALF_HEAD, 1), pltpu.roll(t, HALF_HEAD, 1))
        return t * cos + partner * sin

    q_pieces = ATTN_WIDTH // LANES
    scale = HEAD_DIM ** -0.5 * LOG2_E
    for p in range(q_pieces):
        q_ref[:, p * LANES:(p + 1) * LANES] = (rope(z[p]) * scale).astype(q_ref.dtype)

    keep = seq_block > 0
    for ref in (kext_ref, vext_ref):
        tail = ref[:, rows:rows + BLOCK, :]
        ref[:, 0:BLOCK, :] = jnp.where(keep, tail, jnp.zeros_like(tail))

    k = rope(z[q_pieces])
    v = z[q_pieces + 1]
    k_sw = pltpu.roll(k, HEAD_DIM, 1)
    v_sw = pltpu.roll(v, HEAD_DIM, 1)
    kext_ref[0, BLOCK:, :] = jnp.where(low_half, k, k_sw).astype(kext_ref.dtype)
    kext_ref[1, BLOCK:, :] = jnp.where(low_half, k_sw, k).astype(kext_ref.dtype)
    vext_ref[0, BLOCK:, :] = jnp.where(low_half, v, v_sw).astype(vext_ref.dtype)
    vext_ref[1, BLOCK:, :] = jnp.where(low_half, v_sw, v).astype(vext_ref.dtype)

    halo = utail_ref[...]
    halo = jnp.where(keep, halo, jnp.zeros_like(halo))
    pos = seq_block * rows + lax.broadcasted_iota(jnp.int32, (rows, LANES), 0)
    for gi, size in enumerate(POOL_SIZES):
        cols = slice(gi * LANES, (gi + 1) * LANES)
        u_g = z[q_pieces + 2 + gi]
        utail_ref[:, cols] = u_g[rows - POOL_HALO:]
        acc = jnp.concatenate([halo[:, cols], u_g], axis=0)
        span = 1
        while span < size:
            acc = acc + pltpu.roll(acc, span, 0)
            span *= 2
        count = jnp.minimum(pos + 1, size).astype(jnp.float32)
        mixed = acc[POOL_HALO:] / count - u_g
        y = jnp.dot(mixed.astype(jnp.bfloat16), wpool_ref[gi].astype(jnp.bfloat16),
                    preferred_element_type=jnp.float32)
        pool_ref[:, cols] = ((y + bpool_ref[:, cols]) * pscale_ref[:, cols]).astype(pool_ref.dtype)


def _layer_kernel(sinks_ref, x_hbm, gmix_ref, win32_hbm, bin_ref, wpool_ref, bpool_ref, pscale_ref,
                  wout32_hbm, bout_ref, rcos_ref, rsin_ref, bcos_ref, bsin_ref, gffn_ref, wg32_hbm, wu32_hbm,
                  wd32_hbm, gf_ref, o_hbm,
                  win_ref, wout_ref, wg_ref, wu_ref, wd_ref, win_st, wout_st, wg_st, wu_st, wd_st,
                  x_ring, out_buf, h_ref, q_ref, kext_ref, vext_ref, utail_ref, pool_ref,
                  w_sems, x_sems, o_sems, *, n_blocks, seq_blocks, final_norm):
    rows = x_ring.shape[1]
    chunk_starts = [sum(IN_CHUNKS[:t]) for t in range(len(IN_CHUNKS))]
    n_chunks = len(IN_CHUNKS)
    n_attn = rows // BLOCK

    def x_copy(block, slot):
        start = pl.multiple_of(block * rows, rows)
        return pltpu.make_async_copy(x_hbm.at[pl.ds(start, rows)], x_ring.at[slot], x_sems.at[slot])

    def out_copy(block, slot):
        start = pl.multiple_of(block * rows, rows)
        return pltpu.make_async_copy(out_buf.at[slot], o_hbm.at[pl.ds(start, rows)], o_sems.at[slot])

    weights = ((win32_hbm, win_st, win_ref), (wout32_hbm, wout_st, wout_ref), (wg32_hbm, wg_st, wg_ref),
               (wu32_hbm, wu_st, wu_ref), (wd32_hbm, wd_st, wd_ref))

    def weight_copies(ks, c):
        return [pltpu.make_async_copy(weights[k][0].at[pl.ds(c * weights[k][1].shape[1], weights[k][1].shape[1])],
                                      weights[k][1].at[c % 2], w_sems.at[c % 2, k]) for k in ks]

    def convert_weights(ks):
        for c in range(WEIGHT_CHUNKS):
            for k, cp in zip(ks, weight_copies(ks, c)):
                cp.wait()
                _, stage, dst = weights[k]
                chunk = stage.shape[1]
                dst[c * chunk:(c + 1) * chunk, :] = stage[c % 2].astype(dst.dtype)
            if c + 2 < WEIGHT_CHUNKS:
                for cp in weight_copies(ks, c + 2):
                    cp.start()

    mixer_weights, ffn_weights = (0, 1), (2, 3, 4)
    for b in range(X_SLOTS - 1):
        x_copy(b, b).start()
    for c in range(2):
        for cp in weight_copies(mixer_weights, c):
            cp.start()
    convert_weights(mixer_weights)
    for c in range(2):
        for cp in weight_copies(ffn_weights, c):
            cp.start()

    project_refs = (wpool_ref, bpool_ref, pscale_ref, rcos_ref, rsin_ref, bcos_ref, bsin_ref,
                    q_ref, kext_ref, vext_ref, utail_ref, pool_ref)

    def in_projection_chunk(h, t):
        cols = slice(chunk_starts[t], chunk_starts[t] + IN_CHUNKS[t])
        half = h.shape[0] // 2
        zc = jnp.concatenate([jnp.dot(h[:half], win_ref[:, cols], preferred_element_type=jnp.float32),
                              jnp.dot(h[half:], win_ref[:, cols], preferred_element_type=jnp.float32)],
                             axis=0) + bin_ref[:, cols]
        return [zc[:, c * LANES:(c + 1) * LANES] for c in range(IN_CHUNKS[t] // LANES)]

    kext_ref[...] = jnp.zeros(kext_ref.shape, kext_ref.dtype)
    vext_ref[...] = jnp.zeros(vext_ref.shape, vext_ref.dtype)
    utail_ref[...] = jnp.zeros(utail_ref.shape, utail_ref.dtype)
    x_copy(0, 0).wait()
    h0 = _rmsnorm(x_ring[0], gmix_ref[...]).astype(h_ref.dtype)
    z0 = sum((in_projection_chunk(h0, t) for t in range(n_chunks)), [])
    _store_projection(z0, 0, *project_refs)
    x_copy(1, 1).wait()
    h_ref[...] = _rmsnorm(x_ring[1], gmix_ref[...]).astype(h_ref.dtype)
    convert_weights(ffn_weights)

    def body(i, carry):
        out_slot = i % OUT_SLOTS

        @pl.when(i + 2 <= n_blocks - 1)
        def _():
            x_copy(i + 2, (i + 2) % X_SLOTS).start()

        @pl.when(i + 1 <= n_blocks - 1)
        def _():
            x_copy(i + 1, (i + 1) % X_SLOTS).wait()

        @pl.when(i > OUT_SLOTS)
        def _():
            out_copy(i - 1 - OUT_SLOTS, out_slot).wait()

        prev_seq_block = (i + seq_blocks - 1) % seq_blocks
        x_prev = x_ring.at[(i + X_SLOTS - 1) % X_SLOTS]
        h = h_ref[...]
        z, attn = [], []
        for t in range(max(n_chunks, n_attn)):
            if t < n_attn:
                attn.append(_attention_block(t, prev_seq_block, sinks_ref, q_ref, kext_ref, vext_ref))
            if t < n_chunks:
                z += in_projection_chunk(h, t)

        cat = jnp.concatenate([jnp.concatenate(attn, axis=0).astype(jnp.bfloat16), pool_ref[...]], axis=1)
        x1_chunks, x1_sumsq = _residual_projection(lambda cols: x_prev[:, cols], cat, wout_ref, bout_ref)

        _store_projection(z, i % seq_blocks, *project_refs)

        h_ref[...] = _rmsnorm(x_ring[(i + 1) % X_SLOTS], gmix_ref[...]).astype(h_ref.dtype)

        out_chunks = _ffn_block(x1_chunks, x1_sumsq, gffn_ref, wg_ref, wu_ref, wd_ref, gf_ref, final_norm)
        for c, y in enumerate(out_chunks):
            out_buf[out_slot, :, c * MXU_WIDTH:(c + 1) * MXU_WIDTH] = y

        out_copy(i - 1, out_slot).start()
        return carry

    lax.fori_loop(1, n_blocks + 1, body, 0)

    for i in range(n_blocks + 1 - OUT_SLOTS, n_blocks + 1):
        out_copy(i - 1, i % OUT_SLOTS).wait()


def _rope_tables(rows, seq_blocks):
    inv_freq = 1.0 / (ROPE_THETA ** (jnp.arange(0, HEAD_DIM, 2, dtype=jnp.float32) / HEAD_DIM))
    freq = jnp.tile(inv_freq, LANES // HALF_HEAD)[None, :]
    row_ang = jnp.arange(rows, dtype=jnp.float32)[:, None] * freq
    start_ang = (jnp.arange(seq_blocks, dtype=jnp.float32) * rows)[:, None, None] * freq
    return jnp.cos(row_ang), jnp.sin(row_ang), jnp.cos(start_ang), jnp.sin(start_ang)


def _layer(x, g_mix, w_in, b_in, sinks, w_pool, b_pool, pool_scale, w_out, b_out,
           g_ffn, w_gate, w_up, w_down, g_final, *, final_norm):
    b, s, d = x.shape
    rows = ROWS
    assert s % rows == 0 and rows % BLOCK == 0 and d == D_MODEL and sum(IN_CHUNKS) == IN_WIDTH
    seq_blocks = s // rows
    n = b * seq_blocks
    assert n >= X_SLOTS
    row_cos, row_sin, start_cos, start_sin = _rope_tables(rows, seq_blocks)
    bf16 = jnp.bfloat16
    weights = (w_in, w_out, w_gate, w_up, w_down)
    for w in weights:
        assert w.shape[0] % WEIGHT_CHUNKS == 0 and (w.shape[0] // WEIGHT_CHUNKS) % BF16_SUBLANES == 0

    vmem = pl.BlockSpec(memory_space=pltpu.VMEM)
    hbm = pl.BlockSpec(memory_space=pl.ANY)
    smem = pl.BlockSpec(memory_space=pltpu.SMEM)
    out = pl.pallas_call(
        functools.partial(_layer_kernel, n_blocks=n, seq_blocks=seq_blocks, final_norm=final_norm),
        in_specs=[smem, hbm, vmem, hbm, vmem, vmem, vmem, vmem, hbm, vmem, vmem, vmem, vmem, vmem, vmem,
                  hbm, hbm, hbm, vmem],
        out_specs=hbm,
        out_shape=jax.ShapeDtypeStruct((b * s, d), x.dtype),
        scratch_shapes=(
            [pltpu.VMEM(w.shape, bf16) for w in weights]
            + [pltpu.VMEM((2, w.shape[0] // WEIGHT_CHUNKS, w.shape[1]), jnp.float32) for w in weights]
            + [
                pltpu.VMEM((X_SLOTS, rows, d), jnp.float32),
                pltpu.VMEM((OUT_SLOTS, rows, d), jnp.float32),
                pltpu.VMEM((rows, d), bf16),
                pltpu.VMEM((rows, ATTN_WIDTH), bf16),
                pltpu.VMEM((N_KV_HEADS, rows + BLOCK, LANES), bf16),
                pltpu.VMEM((N_KV_HEADS, rows + BLOCK, LANES), bf16),
                pltpu.VMEM((POOL_HALO, POOL_WIDTH), jnp.float32),
                pltpu.VMEM((rows, POOL_WIDTH), bf16),
                pltpu.SemaphoreType.DMA((2, len(weights))),
                pltpu.SemaphoreType.DMA((X_SLOTS,)),
                pltpu.SemaphoreType.DMA((OUT_SLOTS,)),
            ]),
        compiler_params=pltpu.CompilerParams(vmem_limit_bytes=VMEM_LIMIT_BYTES),
        name="layer",
    )(sinks.astype(jnp.float32), x.reshape(b * s, d), g_mix.reshape(1, d), w_in,
      b_in.reshape(1, IN_WIDTH), w_pool, b_pool.reshape(1, POOL_WIDTH),
      pool_scale.reshape(1, POOL_WIDTH), w_out, b_out.reshape(1, d),
      row_cos, row_sin, start_cos, start_sin,
      g_ffn.reshape(1, d), w_gate, w_up, w_down, g_final.reshape(1, d))
    return out.reshape(b, s, d)


def kernel(x, g_mix, w_in, b_in, sinks, w_pool, b_pool, pool_scale, w_out, b_out, g_ffn, w_gate, w_up, w_down, g_final):
    depth = g_mix.shape[0]
    for layer in range(depth):
        x = _layer(x, g_mix[layer], w_in[layer], b_in[layer], sinks[layer], w_pool[layer],
                   b_pool[layer], pool_scale[layer], w_out[layer], b_out[layer],
                   g_ffn[layer], w_gate[layer], w_up[layer], w_down[layer], g_final,
                   final_norm=layer == depth - 1)
    return x
```
